```python
import math
import jax, jax.numpy as jnp
from jax import lax
import numpy as np

D_MODEL = 2048
BATCH = 4
SEQ = 2048
DEPTH = 2
DEC_BATCH = 128
DEC_SEQ = 4
PAST_LEN = 16384
PAGE_SIZE = 128

N_MIXERS = 2
N_A_LAYERS = (DEPTH + 1) // 2
N_B_LAYERS = DEPTH // 2
A_HEADS = 8
A_DV = D_MODEL // A_HEADS
A_DK = A_DV // 2
A_CHUNK = 64
A_PROJ = 2 * A_HEADS * A_DK + A_HEADS * A_DV + D_MODEL + 2 * A_HEADS
B_CHUNK = 128
B_GROUPS = 8
B_HALF = 3 * D_MODEL
B_GROUP_DIM = B_HALF // B_GROUPS
P_HEADS = 8
P_NKEYS = 128
P_EXPERTS = P_NKEYS * P_NKEYS
P_TOPK = 16
P_DKEY = 256
P_BLOCK = 128
ALPHA = float((2 * DEPTH) ** 0.25)
BETA = float((8 * DEPTH) ** -0.25)
LN_EPS = 1e-5

kernel_name = 'hybrid_mlstm_chunkgmlp_peer_step'


def layer_norm(x, g, b):
    xf = x.astype(jnp.float32)
    mu = jnp.mean(xf, axis=-1, keepdims=True)
    var = jnp.mean(jnp.square(xf - mu), axis=-1, keepdims=True)
    return ((xf - mu) * lax.rsqrt(var + LN_EPS) * g + b).astype(x.dtype)


def mlstm_chunkwise(q, k, v, it, lf, C0, n0, m0, chunk):
    B, T, H, _ = q.shape
    nc = T // chunk
    f32 = jnp.float32

    def to_chunks(a):
        a = a.astype(f32).reshape((B, nc, chunk, H) + a.shape[3:])
        return jnp.moveaxis(a, (1, 3), (0, 2))

    causal = jnp.tril(jnp.ones((chunk, chunk), dtype=bool))

    def step(carry, xs):
        C, n, m = carry
        qc, kc, vc, ic, fc = xs
        b = jnp.cumsum(fc, axis=-1)
        dlog = jnp.where(causal, b[..., :, None] - b[..., None, :] + ic[..., None, :], -jnp.inf)
        inter = b + m[..., None]
        m_t = jnp.maximum(inter, jnp.max(dlog, axis=-1))
        s = jnp.einsum('bhtd,bhsd->bhts', qc, kc) * jnp.exp(dlog - m_t[..., None])
        w_inter = jnp.exp(inter - m_t)
        num = jnp.einsum('bhts,bhse->bhte', s, vc) + w_inter[..., None] * jnp.einsum('bhtd,bhde->bhte', qc, C)
        den = jnp.sum(s, axis=-1) + w_inter * jnp.einsum('bhtd,bhd->bht', qc, n)
        h = num / jnp.maximum(jnp.abs(den), jnp.exp(-m_t))[..., None]
        b_end = b[..., -1]
        g_log = b_end[..., None] - b + ic
        m_new = jnp.maximum(b_end + m, jnp.max(g_log, axis=-1))
        w_k = jnp.exp(g_log - m_new[..., None])
        decay = jnp.exp(b_end + m - m_new)
        C_new = decay[..., None, None] * C + jnp.einsum('bhs,bhsd,bhse->bhde', w_k, kc, vc)
        n_new = decay[..., None] * n + jnp.einsum('bhs,bhsd->bhd', w_k, kc)
        return (C_new, n_new, m_new), h

    xs = (to_chunks(q), to_chunks(k), to_chunks(v), to_chunks(it), to_chunks(lf))
    init = (C0.astype(f32), n0.astype(f32), m0.astype(f32))
    (C, n, m), h = lax.scan(step, init, xs)
    h = jnp.moveaxis(h, (0, 2), (1, 3)).reshape(B, T, H, A_DV)
    return h, C, n, m


def mlstm_mixer(x, C0, n0, m0, w_in, b_gate, hn_gain, w_out):
    B, T, _ = x.shape
    HK, HV = A_HEADS * A_DK, A_HEADS * A_DV
    p = x @ w_in
    q = p[..., :HK].reshape(B, T, A_HEADS, A_DK)
    k = p[..., HK:2 * HK].reshape(B, T, A_HEADS, A_DK) * (A_DK ** -0.5)
    v = p[..., 2 * HK:2 * HK + HV].reshape(B, T, A_HEADS, A_DV)
    o = p[..., 2 * HK + HV:2 * HK + HV + D_MODEL]
    gates = p[..., 2 * HK + HV + D_MODEL:].astype(jnp.float32) + b_gate
    it = gates[..., :A_HEADS]
    lf = jax.nn.log_sigmoid(gates[..., A_HEADS:])
    chunk = math.gcd(A_CHUNK, T)
    h, C, n, m = mlstm_chunkwise(q, k, v, it, lf, C0, n0, m0, chunk)
    h = h * jax.nn.sigmoid(o.astype(jnp.float32)).reshape(B, T, A_HEADS, A_DV)
    mu = jnp.mean(h, axis=-1, keepdims=True)
    var = jnp.mean(jnp.square(h - mu), axis=-1, keepdims=True)
    h = (h - mu) * lax.rsqrt(var + LN_EPS) * hn_gain
    y = h.reshape(B, T, D_MODEL).astype(x.dtype) @ w_out
    return y, C, n, m


def chunk_gmlp_mixer(x, w_in, b_in, lnv_g, lnv_b, w_s, b_s, w_out):
    B, T, _ = x.shape
    z = jax.nn.gelu(x @ w_in + b_in)
    u, v = z[..., :B_HALF], z[..., B_HALF:]
    v = layer_norm(v, lnv_g, lnv_b)
    pad = (-T) % B_CHUNK
    nc = (T + pad) // B_CHUNK
    vc = jnp.pad(v, ((0, 0), (0, pad), (0, 0))).reshape(B, nc, B_CHUNK, B_GROUPS, B_GROUP_DIM)
    ws = jnp.where(jnp.tril(jnp.ones((B_CHUNK, B_CHUNK), dtype=bool)), w_s, jnp.zeros_like(w_s))
    mixed = jnp.einsum('gts,bcsgd->bctgd', ws, vc) + b_s.T[None, None, :, :, None]
    mixed = mixed.reshape(B, T + pad, B_HALF)[:, :T]
    y = (u * mixed) @ w_out
    return y, v


def peer_ffn(x, w_q, sub_keys, exp_u, exp_v):
    B, T, D = x.shape
    N = B * T
    pad = (-N) % P_BLOCK
    xb = jnp.pad(x.reshape(N, D), ((0, pad), (0, 0))).reshape(-1, P_BLOCK, D)
    sk = sub_keys.astype(jnp.float32)

    def block(xt):
        q = (xt @ w_q).astype(jnp.float32).reshape(P_BLOCK, P_HEADS, 2, P_DKEY // 2)
        s = jnp.einsum('bhpd,hpnd->bhpn', q, sk)
        sv, si = lax.top_k(s, P_TOPK)
        cand = (sv[:, :, 0, :, None] + sv[:, :, 1, None, :]).reshape(P_BLOCK, P_HEADS, P_TOPK * P_TOPK)
        cidx = (si[:, :, 0, :, None] * P_NKEYS + si[:, :, 1, None, :]).reshape(P_BLOCK, P_HEADS, P_TOPK * P_TOPK)
        top_s, top_pos = lax.top_k(cand, P_TOPK)
        idx = jnp.take_along_axis(cidx, top_pos, axis=-1)
        g = jax.nn.softmax(top_s, axis=-1)
        a = jnp.einsum('bhkd,bd->bhk', exp_u[idx], xt)
        w = (jax.nn.gelu(a.astype(jnp.float32)) * g).astype(x.dtype)
        return jnp.einsum('bhk,bhkd->bd', w, exp_v[idx])

    out = lax.map(block, xb).reshape(-1, D)[:N]
    return out.reshape(B, T, D)


def trunk(x, C0, n0, m0, w_in_a, b_gate_a, hn_gain_a, w_out_a, w_in_b, b_in_b, lnv_g_b, lnv_b_b,
          w_s_b, b_s_b, w_out_b, ln_mix_g, ln_mix_b, ln_ffn_g, ln_ffn_b,
          peer_w_q, peer_sub_keys, peer_u, peer_v):
    Cs, ns, ms, vs = [], [], [], []
    for i in range(DEPTH):
        j = i // N_MIXERS
        if i % N_MIXERS == 0:
            mix, C, n, m = mlstm_mixer(x, C0[j], n0[j], m0[j], w_in_a[j], b_gate_a[j], hn_gain_a[j], w_out_a[j])
            Cs.append(C)
            ns.append(n)
            ms.append(m)
        else:
            mix, v = chunk_gmlp_mixer(x, w_in_b[j], b_in_b[j], lnv_g_b[j], lnv_b_b[j], w_s_b[j], b_s_b[j], w_out_b[j])
            vs.append(v)
        x = layer_norm(ALPHA * x + mix, ln_mix_g[i], ln_mix_b[i])
        ffn = peer_ffn(x, peer_w_q[i], peer_sub_keys[i], peer_u[i], peer_v[i])
        x = layer_norm(ALPHA * x + ffn, ln_ffn_g[i], ln_ffn_b[i])
    return x, jnp.stack(Cs), jnp.stack(ns), jnp.stack(ms), jnp.stack(vs)


def setup_inputs(seed: int = 0) -> dict:
    key = jax.random.key(seed)
    ks = jax.random.split(key, 28)
    f32 = jnp.float32

    def nrm(k, shape, s):
        return s * jax.random.normal(k, shape, f32)

    b_gate_a = jnp.concatenate([
        nrm(ks[6], (N_A_LAYERS, A_HEADS), 0.1),
        jnp.linspace(3.0, 6.0, A_HEADS, dtype=f32)[None] + nrm(ks[7], (N_A_LAYERS, A_HEADS), 0.1)], axis=-1)
    return {
        'x_prompt': nrm(ks[0], (BATCH, SEQ, D_MODEL), 1.0),
        'x_sample': nrm(ks[1], (DEC_BATCH, DEC_SEQ, D_MODEL), 1.0),
        'state_mlstm_C': nrm(ks[2], (N_A_LAYERS, DEC_BATCH, A_HEADS, A_DK, A_DV), 0.3),
        'state_mlstm_n': nrm(ks[3], (N_A_LAYERS, DEC_BATCH, A_HEADS, A_DK), 0.3),
        'state_mlstm_m': nrm(ks[4], (N_A_LAYERS, DEC_BATCH, A_HEADS), 0.5),
        'w_in_a': nrm(ks[5], (N_A_LAYERS, D_MODEL, A_PROJ), D_MODEL ** -0.5),
        'b_gate_a': b_gate_a,
        'hn_gain_a': 1.0 + nrm(ks[8], (N_A_LAYERS, A_HEADS, A_DV), 0.02),
        'w_out_a': nrm(ks[9], (N_A_LAYERS, D_MODEL, D_MODEL), BETA * D_MODEL ** -0.5),
        'w_in_b': nrm(ks[10], (N_B_LAYERS, D_MODEL, 2 * B_HALF), D_MODEL ** -0.5),
        'b_in_b': nrm(ks[11], (N_B_LAYERS, 2 * B_HALF), 0.02),
        'lnv_g_b': 1.0 + nrm(ks[12], (N_B_LAYERS, B_HALF), 0.02),
        'lnv_b_b': nrm(ks[13], (N_B_LAYERS, B_HALF), 0.02),
        'w_s_b': nrm(ks[14], (N_B_LAYERS, B_GROUPS, B_CHUNK, B_CHUNK), B_CHUNK ** -0.5),
        'b_s_b': 1.0 + nrm(ks[15], (N_B_LAYERS, B_GROUPS, B_CHUNK), 0.02),
        'w_out_b': nrm(ks[16], (N_B_LAYERS, B_HALF, D_MODEL), BETA * B_HALF ** -0.5),
        'ln_mix_g': 1.0 + nrm(ks[17], (DEPTH, D_MODEL), 0.02),
        'ln_mix_b': nrm(ks[18], (DEPTH, D_MODEL), 0.02),
        'ln_ffn_g': 1.0 + nrm(ks[19], (DEPTH, D_MODEL), 0.02),
        'ln_ffn_b': nrm(ks[20], (DEPTH, D_MODEL), 0.02),
        'peer_w_q': nrm(ks[21], (DEPTH, D_MODEL, P_HEADS * P_DKEY), D_MODEL ** -0.5),
        'peer_sub_keys': nrm(ks[22], (DEPTH, P_HEADS, 2, P_NKEYS, P_DKEY // 2), (P_DKEY // 2) ** -0.5),
        'peer_u': nrm(ks[23], (DEPTH, P_EXPERTS, D_MODEL), D_MODEL ** -0.5),
        'peer_v': nrm(ks[24], (DEPTH, P_EXPERTS, D_MODEL), BETA * P_HEADS ** -0.5),
    }


def reference(x_prompt, x_sample, state_mlstm_C, state_mlstm_n, state_mlstm_m,
              w_in_a, b_gate_a, hn_gain_a, w_out_a, w_in_b, b_in_b, lnv_g_b, lnv_b_b,
              w_s_b, b_s_b, w_out_b, ln_mix_g, ln_mix_b, ln_ffn_g, ln_ffn_b,
              peer_w_q, peer_sub_keys, peer_u, peer_v):
    bp = x_prompt.shape[0]
    C0 = jnp.zeros((N_A_LAYERS, bp, A_HEADS, A_DK, A_DV), jnp.float32)
    n0 = jnp.zeros((N_A_LAYERS, bp, A_HEADS, A_DK), jnp.float32)
    m0 = jnp.zeros((N_A_LAYERS, bp, A_HEADS), jnp.float32)
    y_prompt, C_p, n_p, m_p, _ = trunk(
        x_prompt, C0, n0, m0, w_in_a, b_gate_a, hn_gain_a, w_out_a, w_in_b, b_in_b, lnv_g_b, lnv_b_b,
        w_s_b, b_s_b, w_out_b, ln_mix_g, ln_mix_b, ln_ffn_g, ln_ffn_b,
        peer_w_q, peer_sub_keys, peer_u, peer_v)
    y_sample, C_s, n_s, m_s, v_s = trunk(
        x_sample, state_mlstm_C, state_mlstm_n, state_mlstm_m, w_in_a, b_gate_a, hn_gain_a, w_out_a,
        w_in_b, b_in_b, lnv_g_b, lnv_b_b, w_s_b, b_s_b, w_out_b, ln_mix_g, ln_mix_b, ln_ffn_g, ln_ffn_b,
        peer_w_q, peer_sub_keys, peer_u, peer_v)
    return (y_prompt, y_sample, C_p, n_p, m_p, C_s, n_s, m_s, v_s)
```

```python
import functools

import jax
import jax.numpy as jnp
from jax import lax
from jax.experimental import pallas as pl
from jax.experimental.pallas import tpu as pltpu

F32 = jnp.float32
BF16 = jnp.bfloat16

LN_EPS = 1e-5
PEER_TOPK = 16
MLSTM_CHUNK = 256
SAMPLE_PAD_LEN = 8
GATE_LANES = 128
F32_SUBLANES = 8
V7X_VMEM_LIMIT_BYTES = 48 * 1024 * 1024
NEG_BIG = -1e30
POS_BIG = 1e30


def _params(*sem):
    return pltpu.CompilerParams(dimension_semantics=sem,
                                vmem_limit_bytes=V7X_VMEM_LIMIT_BYTES)


def _pick_tile(n, candidates):
    for c in candidates:
        if n % c == 0:
            return c
    raise ValueError(f"no tile in {candidates} divides {n}")


def _layer_norm(z, g, b):
    mu = jnp.mean(z, axis=-1, keepdims=True)
    zc = z - mu
    var = jnp.mean(zc * zc, axis=-1, keepdims=True)
    return zc * lax.rsqrt(var + LN_EPS) * g + b


def _log_sigmoid(x):
    return jnp.minimum(x, 0.0) - jnp.log1p(jnp.exp(-jnp.abs(x)))


def _mm_kernel(x_ref, w_ref, b_ref, o_ref, *, act):
    acc = jnp.dot(x_ref[...], w_ref[...], preferred_element_type=F32)
    acc = acc + b_ref[...]
    if act == "gelu":
        acc = jax.nn.gelu(acc)
    o_ref[...] = acc.astype(o_ref.dtype)


def _matmul(x, w, bias, *, act=None, out_dtype=F32, name):
    m, k = x.shape
    n = w.shape[1]
    tm = _pick_tile(m, (512, 256, 128))
    tn = _pick_tile(n, (1024, 512, 256, 128))
    return pl.pallas_call(
        functools.partial(_mm_kernel, act=act),
        out_shape=jax.ShapeDtypeStruct((m, n), out_dtype),
        grid=(n // tn, m // tm),
        in_specs=[
            pl.BlockSpec((tm, k), lambda j, i: (i, 0)),
            pl.BlockSpec((k, tn), lambda j, i: (0, j)),
            pl.BlockSpec((1, tn), lambda j, i: (0, j)),
        ],
        out_specs=pl.BlockSpec((tm, tn), lambda j, i: (i, j)),
        compiler_params=_params("parallel", "parallel"),
        name=name,
    )(x, w, bias.reshape(1, n).astype(F32))


def _gate_kernel(x_ref, w_ref, b_ref, o_ref):
    x = x_ref[...]
    w = w_ref[...]
    xh = x.astype(BF16)
    xl = (x - xh.astype(F32)).astype(BF16)
    wh = w.astype(BF16)
    wl = (w - wh.astype(F32)).astype(BF16)
    acc = jnp.dot(xh, wh, preferred_element_type=F32)
    acc = acc + jnp.dot(xl, wh, preferred_element_type=F32)
    acc = acc + jnp.dot(xh, wl, preferred_element_type=F32)
    o_ref[...] = acc + b_ref[...]


def _gates(x, w_gate, b_gate):
    m, k = x.shape
    tm = _pick_tile(m, (512, 256, 128))
    return pl.pallas_call(
        _gate_kernel,
        out_shape=jax.ShapeDtypeStruct((m, GATE_LANES), F32),
        grid=(m // tm,),
        in_specs=[
            pl.BlockSpec((tm, k), lambda i: (i, 0)),
            pl.BlockSpec((k, GATE_LANES), lambda i: (0, 0)),
            pl.BlockSpec((1, GATE_LANES), lambda i: (0, 0)),
        ],
        out_specs=pl.BlockSpec((tm, GATE_LANES), lambda i: (i, 0)),
        compiler_params=_params("parallel"),
        name="mlstm_gates",
    )(x, w_gate, b_gate)


def _mlstm_kernel(q_ref, k_ref, v_ref, o_ref, g_ref, gt_ref, c0_ref, n0_ref,
                  m0_ref, gain_ref, hn_ref, c_ref, n_ref, m_ref,
                  *, heads, dk, dv, chunk):
    @pl.when(pl.program_id(1) == 0)
    def _():
        c_ref[...] = c0_ref[...]
        n_ref[...] = n0_ref[...]
        m_ref[...] = m0_ref[...]

    g = g_ref[0]
    gt = gt_ref[0]
    row = lax.broadcasted_iota(jnp.int32, (chunk, chunk), 0)
    col = lax.broadcasted_iota(jnp.int32, (chunk, chunk), 1)
    causal = row >= col
    tril = causal.astype(F32)
    triu = (row <= col).astype(F32)
    lf_col = _log_sigmoid(g)
    lf_row = _log_sigmoid(gt)
    b_col_all = jnp.dot(tril, lf_col, precision=lax.Precision.HIGHEST,
                        preferred_element_type=F32)
    b_row_all = jnp.dot(lf_row, triu, precision=lax.Precision.HIGHEST,
                        preferred_element_type=F32)
    k_scale = dk ** -0.5

    for h in range(heads):
        bc = b_col_all[:, heads + h:heads + h + 1]
        br = b_row_all[heads + h:heads + h + 1, :]
        ic = g[:, h:h + 1]
        ir = gt[h:h + 1, :]
        m_prev = m_ref[0, h][:, 0:1]
        c_prev = c_ref[0, h]
        n_prev = n_ref[0, h]

        dlog = jnp.where(causal, bc - br + ir, -jnp.inf)
        inter = bc + m_prev
        m_t = jnp.maximum(inter, jnp.max(dlog, axis=-1, keepdims=True))
        decay_mat = jnp.exp(dlog - m_t)
        w_inter = jnp.exp(inter - m_t)

        qh = q_ref[0, :, h * dk:(h + 1) * dk]
        kh = k_ref[0, :, h * dk:(h + 1) * dk] * k_scale
        vh = v_ref[0, :, h * dv:(h + 1) * dv]
        qb = qh.astype(BF16)
        kb = kh.astype(BF16)
        vb = vh.astype(BF16)
        qk = lax.dot_general(qb, kb, (((1,), (1,)), ((), ())),
                             preferred_element_type=F32)
        s = qk * decay_mat
        num = jnp.dot(s.astype(BF16), vb, preferred_element_type=F32)
        num = num + w_inter * jnp.dot(qb, c_prev.astype(BF16),
                                      preferred_element_type=F32)
        den = jnp.sum(s, axis=-1, keepdims=True)
        den = den + w_inter * jnp.sum(qh * n_prev, axis=-1, keepdims=True)
        hval = num / jnp.maximum(jnp.abs(den), jnp.exp(-m_t))

        hg = hval * jax.nn.sigmoid(o_ref[0, :, h * dv:(h + 1) * dv])
        mu = jnp.mean(hg, axis=-1, keepdims=True)
        hc = hg - mu
        var = jnp.mean(hc * hc, axis=-1, keepdims=True)
        hn = hc * lax.rsqrt(var + LN_EPS) * gain_ref[h]
        hn_ref[0, :, h * dv:(h + 1) * dv] = hn.astype(hn_ref.dtype)

        b_end = bc[chunk - 1:chunk, :]
        glog_c = b_end - bc + ic
        glog_r = b_end - br + ir
        m_new = jnp.maximum(b_end + m_prev,
                            jnp.max(glog_r, axis=-1, keepdims=True))
        w_k = jnp.exp(glog_c - m_new)
        decay = jnp.exp(b_end + m_prev - m_new)
        kw = kh * w_k
        c_new = decay * c_prev + lax.dot_general(
            kw.astype(BF16), vb, (((0,), (0,)), ((), ())),
            preferred_element_type=F32)
        n_new = decay * n_prev + jnp.sum(kw, axis=0, keepdims=True)
        c_ref[0, h] = c_new
        n_ref[0, h] = n_new
        m_ref[0, h] = jnp.broadcast_to(m_new, (1, GATE_LANES))


def _mlstm(proj, gates, c0, n0, m0, gain, *, heads, dk, dv, chunk, name):
    bsz, t, _ = proj.shape
    hk, hv = heads * dk, heads * dv
    gt = jnp.swapaxes(gates[..., :2 * heads], 1, 2)
    n0 = n0.reshape(bsz, heads, 1, dk)
    m0 = jnp.broadcast_to(m0[..., None, None], (bsz, heads, 1, GATE_LANES))
    state_spec = lambda shape: pl.BlockSpec((1,) + shape, lambda b, c: (b, 0, 0, 0))
    hn, c, n, m = pl.pallas_call(
        functools.partial(_mlstm_kernel, heads=heads, dk=dk, dv=dv, chunk=chunk),
        out_shape=(
            jax.ShapeDtypeStruct((bsz, t, hv), BF16),
            jax.ShapeDtypeStruct((bsz, heads, dk, dv), F32),
            jax.ShapeDtypeStruct((bsz, heads, 1, dk), F32),
            jax.ShapeDtypeStruct((bsz, heads, 1, GATE_LANES), F32),
        ),
        grid=(bsz, t // chunk),
        in_specs=[
            pl.BlockSpec((1, chunk, hk), lambda b, c: (b, c, 0)),
            pl.BlockSpec((1, chunk, hk), lambda b, c: (b, c, 1)),
            pl.BlockSpec((1, chunk, hv), lambda b, c: (b, c, 1)),
            pl.BlockSpec((1, chunk, hv), lambda b, c: (b, c, 2)),
            pl.BlockSpec((1, chunk, GATE_LANES), lambda b, c: (b, c, 0)),
            pl.BlockSpec((1, 2 * heads, chunk), lambda b, c: (b, 0, c)),
            state_spec((heads, dk, dv)),
            state_spec((heads, 1, dk)),
            state_spec((heads, 1, GATE_LANES)),
            pl.BlockSpec((heads, 1, dv), lambda b, c: (0, 0, 0)),
        ],
        out_specs=(
            pl.BlockSpec((1, chunk, hv), lambda b, c: (b, c, 0)),
            state_spec((heads, dk, dv)),
            state_spec((heads, 1, dk)),
            state_spec((heads, 1, GATE_LANES)),
        ),
        compiler_params=_params("parallel", "arbitrary"),
        name=name,
    )(proj, proj, proj, proj, gates, gt, c0, n0, m0, gain.reshape(heads, 1, dv))
    return hn, c, n.reshape(bsz, heads, dk), m[:, :, 0, 0]


def _mm_res_ln_kernel(a_ref, w_ref, x_ref, g_ref, b_ref, o_ref, ob_ref, *, alpha):
    y = jnp.dot(a_ref[...], w_ref[...], preferred_element_type=F32)
    out = _layer_norm(alpha * x_ref[...] + y, g_ref[...], b_ref[...])
    o_ref[...] = out
    ob_ref[...] = out.astype(BF16)


def _matmul_res_ln(a, w, x, g, b, *, alpha, name):
    m, k = a.shape
    d = w.shape[1]
    tm = _pick_tile(m, (256, 128))
    row = lambda i: (i, 0)
    const = lambda i: (0, 0)
    return pl.pallas_call(
        functools.partial(_mm_res_ln_kernel, alpha=alpha),
        out_shape=(jax.ShapeDtypeStruct((m, d), F32),
                   jax.ShapeDtypeStruct((m, d), BF16)),
        grid=(m // tm,),
        in_specs=[
            pl.BlockSpec((tm, k), row),
            pl.BlockSpec((k, d), const, pipeline_mode=pl.Buffered(1)),
            pl.BlockSpec((tm, d), row),
            pl.BlockSpec((1, d), const),
            pl.BlockSpec((1, d), const),
        ],
        out_specs=(pl.BlockSpec((tm, d), row), pl.BlockSpec((tm, d), row)),
        compiler_params=_params("parallel"),
        name=name,
    )(a, w, x, g.reshape(1, d), b.reshape(1, d))


def _res_ln_kernel(x_ref, y_ref, g_ref, b_ref, o_ref, ob_ref, *, alpha):
    out = _layer_norm(alpha * x_ref[...] + y_ref[...], g_ref[...], b_ref[...])
    o_ref[...] = out
    ob_ref[...] = out.astype(BF16)


def _res_ln(x, y, g, b, *, alpha, name):
    m, d = x.shape
    tm = _pick_tile(m, (512, 256, 128))
    row = lambda i: (i, 0)
    const = lambda i: (0, 0)
    return pl.pallas_call(
        functools.partial(_res_ln_kernel, alpha=alpha),
        out_shape=(jax.ShapeDtypeStruct((m, d), F32),
                   jax.ShapeDtypeStruct((m, d), BF16)),
        grid=(m // tm,),
        in_specs=[pl.BlockSpec((tm, d), row), pl.BlockSpec((tm, d), row),
                  pl.BlockSpec((1, d), const), pl.BlockSpec((1, d), const)],
        out_specs=(pl.BlockSpec((tm, d), row), pl.BlockSpec((tm, d), row)),
        compiler_params=_params("parallel"),
        name=name,
    )(x, y, g.reshape(1, d), b.reshape(1, d))


def _gmlp_mix_kernel(u_ref, v_ref, wmix_ref, bmix_ref, lg_ref, lb_ref,
                     prod_ref, vln_ref, *, groups, gdim):
    vln = _layer_norm(v_ref[...], lg_ref[...], lb_ref[...])
    vln_ref[...] = vln
    for g in range(groups):
        sl = slice(g * gdim, (g + 1) * gdim)
        mixed = jnp.dot(wmix_ref[0, g].astype(BF16), vln[:, sl].astype(BF16),
                        preferred_element_type=F32)
        mixed = mixed + bmix_ref[0, :, g:g + 1]
        prod_ref[:, sl] = (u_ref[:, sl].astype(F32) * mixed).astype(prod_ref.dtype)


def _gmlp_mix(u, v, wmix, bmix, lnv_g, lnv_b, *, n_prompt_chunks, n_sample_rows):
    n, half = v.shape
    groups, chunk = wmix.shape[1], wmix.shape[2]
    gdim = half // groups
    kind = lambda i: jnp.where(i >= n_prompt_chunks, 1, 0)
    row = lambda i: (i, 0)
    const = lambda i: (0, 0)
    return pl.pallas_call(
        functools.partial(_gmlp_mix_kernel, groups=groups, gdim=gdim),
        out_shape=(jax.ShapeDtypeStruct((n, half), BF16),
                   jax.ShapeDtypeStruct((n_sample_rows, half), F32)),
        grid=(n // chunk,),
        in_specs=[
            pl.BlockSpec((chunk, half), row),
            pl.BlockSpec((chunk, half), row),
            pl.BlockSpec((1, groups, chunk, chunk), lambda i: (kind(i), 0, 0, 0)),
            pl.BlockSpec((1, chunk, groups), lambda i: (kind(i), 0, 0)),
            pl.BlockSpec((1, half), const),
            pl.BlockSpec((1, half), const),
        ],
        out_specs=(
            pl.BlockSpec((chunk, half), row),
            pl.BlockSpec((chunk, half),
                         lambda i: (jnp.maximum(i - n_prompt_chunks, 0), 0)),
        ),
        compiler_params=_params("arbitrary"),
        name="gmlp_mix",
    )(u, v, wmix, bmix, lnv_g.reshape(1, half), lnv_b.reshape(1, half))


def _top_values(s, count):
    cur = jnp.full((1, s.shape[1]), jnp.inf, F32)
    out = []
    for _ in range(count):
        cur = jnp.max(jnp.where(s < cur, s, -jnp.inf), axis=0, keepdims=True)
        out.append(cur)
    return out


def _peer_select_kernel(xt_ref, wqt_ref, sk_ref, s0_ref, s1_ref, pz0_ref, p1_ref,
                        thr_ref, qt_ref, tops_ref, *, heads, nkeys):
    qt_ref[...] = jnp.dot(wqt_ref[...], xt_ref[...], preferred_element_type=F32)
    score_refs = (s0_ref, s1_ref)
    for h in range(heads):
        for p in range(2):
            r0 = (2 * h + p) * nkeys
            qhp = qt_ref[r0:r0 + nkeys, :].astype(BF16)
            s = jnp.dot(sk_ref[h, p], qhp, preferred_element_type=F32)
            score_refs[p][h] = s
            for r, top in enumerate(_top_values(s, PEER_TOPK)):
                tops_ref[p, r, h:h + 1, :] = top

    a = [tops_ref[0, r] for r in range(PEER_TOPK)]
    b = [tops_ref[1, r] for r in range(PEER_TOPK)]
    cands = [a[k] + b[l] for k in range(PEER_TOPK) for l in range(PEER_TOPK)
             if (k + 1) * (l + 1) <= PEER_TOPK]
    cur = jnp.full(cands[0].shape, jnp.inf, F32)
    for _ in range(PEER_TOPK):
        nxt = jnp.full(cands[0].shape, -jnp.inf, F32)
        for c in cands:
            nxt = jnp.maximum(nxt, jnp.where(c < cur, c, -jnp.inf))
        cur = nxt
    thr = cur
    top = cands[0]
    z = jnp.zeros(thr.shape, F32)
    for c in cands:
        z = z + jnp.where(c >= thr, jnp.exp(c - top), 0.0)
    thr_ref[...] = thr
    inv_z = 1.0 / z
    for h in range(heads):
        pz0_ref[h] = jnp.exp(s0_ref[h] - a[0][h:h + 1, :]) * inv_z[h:h + 1, :]
        p1_ref[h] = jnp.exp(s1_ref[h] - b[0][h:h + 1, :])


def _peer_select(xt, wqt, sk):
    d, n = xt.shape
    heads, _, nkeys, _ = sk.shape
    tm = _pick_tile(n, (256, 128))
    per_key = jax.ShapeDtypeStruct((heads, nkeys, n), F32)
    key_spec = pl.BlockSpec((heads, nkeys, tm), lambda t: (0, 0, t))
    return pl.pallas_call(
        functools.partial(_peer_select_kernel, heads=heads, nkeys=nkeys),
        out_shape=(per_key, per_key, per_key, per_key,
                   jax.ShapeDtypeStruct((heads, n), F32)),
        grid=(n // tm,),
        in_specs=[
            pl.BlockSpec((d, tm), lambda t: (0, t)),
            pl.BlockSpec(wqt.shape, lambda t: (0, 0), pipeline_mode=pl.Buffered(1)),
            pl.BlockSpec(sk.shape, lambda t: (0, 0, 0, 0)),
        ],
        out_specs=(key_spec, key_spec, key_spec, key_spec,
                   pl.BlockSpec((heads, tm), lambda t: (0, t))),
        scratch_shapes=[pltpu.VMEM((wqt.shape[0], tm), F32),
                        pltpu.VMEM((2, PEER_TOPK, heads, tm), F32)],
        compiler_params=_params("parallel"),
        name="peer_select",
    )(xt, wqt, sk)


def _peer_expert_kernel(xt_ref, u_ref, vt_ref, s0_ref, s1_ref, pz0_ref, p1_ref,
                        thr_ref, out_ref, at_ref, wt_ref, *, heads, nkeys, lane_tile):
    e = pl.program_id(1)
    eb, tm = at_ref.shape
    rows_per_block = eb // nkeys

    @pl.when(e == 0)
    def _():
        out_ref[...] = jnp.zeros_like(out_ref)

    at_ref[...] = jnp.dot(u_ref[...], xt_ref[...], preferred_element_type=F32)
    first_row = pl.multiple_of(e * rows_per_block, rows_per_block)
    for tc in range(tm // lane_tile):
        lanes = slice(tc * lane_tile, (tc + 1) * lane_tile)
        s0_rows = [s0_ref[h, pl.ds(first_row, rows_per_block), lanes] for h in range(heads)]
        pz0_rows = [pz0_ref[h, pl.ds(first_row, rows_per_block), lanes] for h in range(heads)]
        for ii in range(rows_per_block):
            g = jnp.zeros((nkeys, lane_tile), F32)
            for h in range(heads):
                s0_row = s0_rows[h][ii:ii + 1, :]
                pz0_row = pz0_rows[h][ii:ii + 1, :]
                thr_row = thr_ref[h:h + 1, lanes]
                pair = s1_ref[h, :, lanes] + s0_row
                g = g + jnp.where(pair >= thr_row, p1_ref[h, :, lanes] * pz0_row, 0.0)
            a = at_ref[ii * nkeys:(ii + 1) * nkeys, lanes]
            wt_ref[ii * nkeys:(ii + 1) * nkeys, lanes] = (jax.nn.gelu(a) * g).astype(BF16)
    out_ref[...] += jnp.dot(vt_ref[...], wt_ref[...], preferred_element_type=F32)


def _peer_expert(xt, u, vt, s0, s1, pz0, p1, thr):
    d, n = xt.shape
    n_exp = u.shape[0]
    heads, nkeys, _ = s0.shape
    tm = _pick_tile(n, (512, 256, 128))
    eb = F32_SUBLANES * nkeys
    once = pl.Buffered(1)
    key_spec = pl.BlockSpec((heads, nkeys, tm), lambda t, e: (0, 0, t), pipeline_mode=once)
    return pl.pallas_call(
        functools.partial(_peer_expert_kernel, heads=heads, nkeys=nkeys, lane_tile=128),
        out_shape=jax.ShapeDtypeStruct((d, n), F32),
        grid=(n // tm, n_exp // eb),
        in_specs=[
            pl.BlockSpec((d, tm), lambda t, e: (0, t), pipeline_mode=once),
            pl.BlockSpec((eb, d), lambda t, e: (e, 0)),
            pl.BlockSpec((d, eb), lambda t, e: (0, e)),
            key_spec, key_spec, key_spec, key_spec,
            pl.BlockSpec((heads, tm), lambda t, e: (0, t), pipeline_mode=once),
        ],
        out_specs=pl.BlockSpec((d, tm), lambda t, e: (0, t)),
        scratch_shapes=[pltpu.VMEM((eb, tm), F32), pltpu.VMEM((eb, tm), BF16)],
        compiler_params=_params("parallel", "arbitrary"),
        name="peer_expert",
    )(xt, u, vt, s0, s1, pz0, p1, thr)


def _peer(x, xb, w_q, sub_keys, exp_u, exp_v, ln_g, ln_b, *, alpha, layer):
    xt = xb.T
    wqt = w_q.T.astype(BF16)
    sk = sub_keys.astype(BF16)
    s0, s1, pz0, p1, thr = _peer_select(xt, wqt, sk)
    out_t = _peer_expert(xt, exp_u.astype(BF16), exp_v.T.astype(BF16),
                         s0, s1, pz0, p1, thr)
    return _res_ln(x, out_t.T, ln_g, ln_b, alpha=alpha, name=f"peer_ln_{layer}")


def kernel(x_prompt, x_sample, state_mlstm_C, state_mlstm_n, state_mlstm_m,
           w_in_a, b_gate_a, hn_gain_a, w_out_a, w_in_b, b_in_b, lnv_g_b, lnv_b_b,
           w_s_b, b_s_b, w_out_b, ln_mix_g, ln_mix_b, ln_ffn_g, ln_ffn_b,
           peer_w_q, peer_sub_keys, peer_u, peer_v):
    bp, tp, d = x_prompt.shape
    bs, ts, _ = x_sample.shape
    depth = ln_mix_g.shape[0]
    alpha = float((2 * depth) ** 0.25)
    heads = b_gate_a.shape[-1] // 2
    dv = hn_gain_a.shape[-1]
    dk = state_mlstm_n.shape[-1]
    hk, hv = heads * dk, heads * dv
    n_prompt, n_sample = bp * tp, bs * ts
    assert depth == 2 and w_in_a.shape[0] == 1 and w_in_b.shape[0] == 1
    assert ts <= SAMPLE_PAD_LEN and tp % MLSTM_CHUNK == 0

    x = jnp.concatenate([x_prompt.reshape(n_prompt, d), x_sample.reshape(n_sample, d)])
    xb = x.astype(BF16)

    n_main = 2 * hk + 2 * hv
    proj = _matmul(xb, w_in_a[0][:, :n_main].astype(BF16), jnp.zeros((n_main,), F32),
                   name="mlstm_in_proj")
    w_gate = jnp.pad(w_in_a[0][:, n_main:], ((0, 0), (0, GATE_LANES - 2 * heads)))
    b_gate = jnp.pad(b_gate_a[0], (0, GATE_LANES - 2 * heads)).reshape(1, GATE_LANES)
    gates = _gates(x, w_gate, b_gate)

    zeros_c = jnp.zeros((bp, heads, dk, dv), F32)
    hn_p, c_p, n_p, m_p = _mlstm(
        proj[:n_prompt].reshape(bp, tp, n_main), gates[:n_prompt].reshape(bp, tp, GATE_LANES),
        zeros_c, jnp.zeros((bp, heads, dk), F32), jnp.zeros((bp, heads), F32),
        hn_gain_a[0], heads=heads, dk=dk, dv=dv, chunk=MLSTM_CHUNK, name="mlstm_prompt")

    pad_t = SAMPLE_PAD_LEN - ts
    proj_s = jnp.pad(proj[n_prompt:].reshape(bs, ts, n_main), ((0, 0), (0, pad_t), (0, 0)))
    lane = jnp.arange(GATE_LANES)
    pad_gate = jnp.where(lane < heads, NEG_BIG, jnp.where(lane < 2 * heads, POS_BIG, 0.0))
    gates_s = jnp.concatenate(
        [gates[n_prompt:].reshape(bs, ts, GATE_LANES),
         jnp.broadcast_to(pad_gate.astype(F32), (bs, pad_t, GATE_LANES))], axis=1)
    hn_s, c_s, n_s, m_s = _mlstm(
        proj_s, gates_s, state_mlstm_C[0], state_mlstm_n[0], state_mlstm_m[0],
        hn_gain_a[0], heads=heads, dk=dk, dv=dv, chunk=SAMPLE_PAD_LEN, name="mlstm_sample")

    hn = jnp.concatenate([hn_p.reshape(n_prompt, hv), hn_s[:, :ts].reshape(n_sample, hv)])
    x, xb = _matmul_res_ln(hn, w_out_a[0].astype(BF16), x, ln_mix_g[0], ln_mix_b[0],
                           alpha=alpha, name="mlstm_out_proj_ln")
    x, xb = _peer(x, xb, peer_w_q[0], peer_sub_keys[0], peer_u[0], peer_v[0],
                  ln_ffn_g[0], ln_ffn_b[0], alpha=alpha, layer=0)

    half = w_in_b.shape[-1] // 2
    groups, chunk = w_s_b.shape[1], w_s_b.shape[2]
    assert tp % chunk == 0 and n_sample % chunk == 0 and chunk % ts == 0
    w_in = w_in_b[0].astype(BF16)
    u = _matmul(xb, w_in[:, :half], b_in_b[0][:half], act="gelu", out_dtype=BF16,
                name="gmlp_in_proj_u")
    v = _matmul(xb, w_in[:, half:], b_in_b[0][half:], act="gelu", name="gmlp_in_proj_v")
    ws = jnp.where(jnp.tril(jnp.ones((chunk, chunk), dtype=bool)), w_s_b[0], 0.0)
    ws_sample = jnp.einsum("ab,gts->gatbs", jnp.eye(chunk // ts, dtype=F32),
                           ws[:, :ts, :ts]).reshape(groups, chunk, chunk)
    wmix = jnp.stack([ws, ws_sample])
    bmix = jnp.stack([b_s_b[0].T, jnp.tile(b_s_b[0][:, :ts].T, (chunk // ts, 1))])
    prod, v_s = _gmlp_mix(u, v, wmix, bmix, lnv_g_b[0], lnv_b_b[0],
                          n_prompt_chunks=n_prompt // chunk, n_sample_rows=n_sample)
    x, xb = _matmul_res_ln(prod, w_out_b[0].astype(BF16), x, ln_mix_g[1], ln_mix_b[1],
                           alpha=alpha, name="gmlp_out_proj_ln")
    x, xb = _peer(x, xb, peer_w_q[1], peer_sub_keys[1], peer_u[1], peer_v[1],
                  ln_ffn_g[1], ln_ffn_b[1], alpha=alpha, layer=1)

    return (x[:n_prompt].reshape(bp, tp, d), x[n_prompt:].reshape(bs, ts, d),
            c_p[None], n_p[None], m_p[None], c_s[None], n_s[None], m_s[None],
            v_s.reshape(1, bs, ts, half))
```

```python
import functools

import jax
import jax.numpy as jnp
from jax import lax
from jax.experimental import pallas as pl
from jax.experimental.pallas import tpu as pltpu

F32 = jnp.float32
BF16 = jnp.bfloat16

LN_EPS = 1e-5
PEER_TOPK = 16
MLSTM_CHUNK = 256
SAMPLE_PAD_LEN = 8
GATE_LANES = 128
LANES = 128
F32_SUBLANES = 8
BF16_SUBLANES = 16
NOT_RANKED = 255.0
V7X_VMEM_LIMIT_BYTES = 48 * 1024 * 1024
NEG_BIG = -1e30
POS_BIG = 1e30


def _params(*sem):
    return pltpu.CompilerParams(dimension_semantics=sem,
                                vmem_limit_bytes=V7X_VMEM_LIMIT_BYTES)


def _pick_tile(n, candidates):
    for c in candidates:
        if n % c == 0:
            return c
    raise ValueError(f"no tile in {candidates} divides {n}")


def _layer_norm(z, g, b):
    mu = jnp.mean(z, axis=-1, keepdims=True)
    zc = z - mu
    var = jnp.mean(zc * zc, axis=-1, keepdims=True)
    return zc * lax.rsqrt(var + LN_EPS) * g + b


def _log_sigmoid(x):
    return jnp.minimum(x, 0.0) - jnp.log1p(jnp.exp(-jnp.abs(x)))


def _mm_kernel(x_ref, w_ref, b_ref, o_ref, *, act):
    acc = jnp.dot(x_ref[...], w_ref[...], preferred_element_type=F32)
    acc = acc + b_ref[...]
    if act == "gelu":
        acc = jax.nn.gelu(acc)
    o_ref[...] = acc.astype(o_ref.dtype)


def _matmul(x, w, bias, *, act=None, out_dtype=F32, name):
    m, k = x.shape
    n = w.shape[1]
    tm = _pick_tile(m, (512, 256, 128))
    tn = _pick_tile(n, (1024, 512, 256, 128))
    return pl.pallas_call(
        functools.partial(_mm_kernel, act=act),
        out_shape=jax.ShapeDtypeStruct((m, n), out_dtype),
        grid=(n // tn, m // tm),
        in_specs=[
            pl.BlockSpec((tm, k), lambda j, i: (i, 0)),
            pl.BlockSpec((k, tn), lambda j, i: (0, j)),
            pl.BlockSpec((1, tn), lambda j, i: (0, j)),
        ],
        out_specs=pl.BlockSpec((tm, tn), lambda j, i: (i, j)),
        compiler_params=_params("parallel", "parallel"),
        name=name,
    )(x, w, bias.reshape(1, n).astype(F32))


def _gate_kernel(x_ref, w_ref, b_ref, o_ref):
    x = x_ref[...]
    w = w_ref[...]
    xh = x.astype(BF16)
    xl = (x - xh.astype(F32)).astype(BF16)
    wh = w.astype(BF16)
    wl = (w - wh.astype(F32)).astype(BF16)
    acc = jnp.dot(xh, wh, preferred_element_type=F32)
    acc = acc + jnp.dot(xl, wh, preferred_element_type=F32)
    acc = acc + jnp.dot(xh, wl, preferred_element_type=F32)
    o_ref[...] = acc + b_ref[...]


def _gates(x, w_gate, b_gate):
    m, k = x.shape
    tm = _pick_tile(m, (512, 256, 128))
    return pl.pallas_call(
        _gate_kernel,
        out_shape=jax.ShapeDtypeStruct((m, GATE_LANES), F32),
        grid=(m // tm,),
        in_specs=[
            pl.BlockSpec((tm, k), lambda i: (i, 0)),
            pl.BlockSpec((k, GATE_LANES), lambda i: (0, 0)),
            pl.BlockSpec((1, GATE_LANES), lambda i: (0, 0)),
        ],
        out_specs=pl.BlockSpec((tm, GATE_LANES), lambda i: (i, 0)),
        compiler_params=_params("parallel"),
        name="mlstm_gates",
    )(x, w_gate, b_gate)


def _mlstm_kernel(q_ref, k_ref, v_ref, o_ref, g_ref, gt_ref, c0_ref, n0_ref,
                  m0_ref, gain_ref, hn_ref, c_ref, n_ref, m_ref,
                  *, heads, dk, dv, chunk):
    @pl.when(pl.program_id(1) == 0)
    def _():
        c_ref[...] = c0_ref[...]
        n_ref[...] = n0_ref[...]
        m_ref[...] = m0_ref[...]

    g = g_ref[0]
    gt = gt_ref[0]
    row = lax.broadcasted_iota(jnp.int32, (chunk, chunk), 0)
    col = lax.broadcasted_iota(jnp.int32, (chunk, chunk), 1)
    causal = row >= col
    tril = causal.astype(F32)
    triu = (row <= col).astype(F32)
    lf_col = _log_sigmoid(g)
    lf_row = _log_sigmoid(gt)
    b_col_all = jnp.dot(tril, lf_col, precision=lax.Precision.HIGHEST,
                        preferred_element_type=F32)
    b_row_all = jnp.dot(lf_row, triu, precision=lax.Precision.HIGHEST,
                        preferred_element_type=F32)
    k_scale = dk ** -0.5

    for h in range(heads):
        bc = b_col_all[:, heads + h:heads + h + 1]
        br = b_row_all[heads + h:heads + h + 1, :]
        ic = g[:, h:h + 1]
        ir = gt[h:h + 1, :]
        m_prev = m_ref[0, h][:, 0:1]
        c_prev = c_ref[0, h]
        n_prev = n_ref[0, h]

        dlog = jnp.where(causal, bc - br + ir, -jnp.inf)
        inter = bc + m_prev
        m_t = jnp.maximum(inter, jnp.max(dlog, axis=-1, keepdims=True))
        decay_mat = jnp.exp(dlog - m_t)
        w_inter = jnp.exp(inter - m_t)

        qh = q_ref[0, :, h * dk:(h + 1) * dk]
        kh = k_ref[0, :, h * dk:(h + 1) * dk] * k_scale
        vh = v_ref[0, :, h * dv:(h + 1) * dv]
        qb = qh.astype(BF16)
        kb = kh.astype(BF16)
        vb = vh.astype(BF16)
        qk = lax.dot_general(qb, kb, (((1,), (1,)), ((), ())),
                             preferred_element_type=F32)
        s = qk * decay_mat
        num = jnp.dot(s.astype(BF16), vb, preferred_element_type=F32)
        num = num + w_inter * jnp.dot(qb, c_prev.astype(BF16),
                                      preferred_element_type=F32)
        den = jnp.sum(s, axis=-1, keepdims=True)
        den = den + w_inter * jnp.sum(qh * n_prev, axis=-1, keepdims=True)
        hval = num / jnp.maximum(jnp.abs(den), jnp.exp(-m_t))

        hg = hval * jax.nn.sigmoid(o_ref[0, :, h * dv:(h + 1) * dv])
        mu = jnp.mean(hg, axis=-1, keepdims=True)
        hc = hg - mu
        var = jnp.mean(hc * hc, axis=-1, keepdims=True)
        hn = hc * lax.rsqrt(var + LN_EPS) * gain_ref[h]
        hn_ref[0, :, h * dv:(h + 1) * dv] = hn.astype(hn_ref.dtype)

        b_end = bc[chunk - 1:chunk, :]
        glog_c = b_end - bc + ic
        glog_r = b_end - br + ir
        m_new = jnp.maximum(b_end + m_prev,
                            jnp.max(glog_r, axis=-1, keepdims=True))
        w_k = jnp.exp(glog_c - m_new)
        decay = jnp.exp(b_end + m_prev - m_new)
        kw = kh * w_k
        c_new = decay * c_prev + lax.dot_general(
            kw.astype(BF16), vb, (((0,), (0,)), ((), ())),
            preferred_element_type=F32)
        n_new = decay * n_prev + jnp.sum(kw, axis=0, keepdims=True)
        c_ref[0, h] = c_new
        n_ref[0, h] = n_new
        m_ref[0, h] = jnp.broadcast_to(m_new, (1, GATE_LANES))


def _mlstm(proj, gates, c0, n0, m0, gain, *, bsz, seq_len, heads, dk, dv, chunk, name):
    hk, hv = heads * dk, heads * dv
    chunks = seq_len // chunk
    if proj.shape[0] == 1:
        rows = lambda b, c: (0, b * chunks + c)
    else:
        rows = lambda b, c: (b, c)
    gt = jnp.swapaxes(gates[..., :2 * heads], 1, 2)
    n0 = n0.reshape(bsz, heads, 1, dk)
    m0 = jnp.broadcast_to(m0[..., None, None], (bsz, heads, 1, GATE_LANES))
    state_spec = lambda shape: pl.BlockSpec((1,) + shape, lambda b, c: (b, 0, 0, 0))
    tok_spec = lambda width, col: pl.BlockSpec((1, chunk, width), lambda b, c: rows(b, c) + (col,))
    hn, c, n, m = pl.pallas_call(
        functools.partial(_mlstm_kernel, heads=heads, dk=dk, dv=dv, chunk=chunk),
        out_shape=(
            jax.ShapeDtypeStruct((proj.shape[0], bsz * seq_len // proj.shape[0], hv), BF16),
            jax.ShapeDtypeStruct((bsz, heads, dk, dv), F32),
            jax.ShapeDtypeStruct((bsz, heads, 1, dk), F32),
            jax.ShapeDtypeStruct((bsz, heads, 1, GATE_LANES), F32),
        ),
        grid=(bsz, chunks),
        in_specs=[
            tok_spec(hk, 0),
            tok_spec(hk, 1),
            tok_spec(hv, 1),
            tok_spec(hv, 2),
            tok_spec(GATE_LANES, 0),
            pl.BlockSpec((1, 2 * heads, chunk),
                         lambda b, c: (rows(b, c)[0], 0, rows(b, c)[1])),
            state_spec((heads, dk, dv)),
            state_spec((heads, 1, dk)),
            state_spec((heads, 1, GATE_LANES)),
            pl.BlockSpec((heads, 1, dv), lambda b, c: (0, 0, 0)),
        ],
        out_specs=(
            tok_spec(hv, 0),
            state_spec((heads, dk, dv)),
            state_spec((heads, 1, dk)),
            state_spec((heads, 1, GATE_LANES)),
        ),
        compiler_params=_params("parallel", "arbitrary"),
        name=name,
    )(proj, proj, proj, proj, gates, gt, c0, n0, m0, gain.reshape(heads, 1, dv))
    return hn, c, n.reshape(bsz, heads, dk), m[:, :, 0, 0]


def _mm_res_ln_kernel(a_ref, w_ref, x_ref, g_ref, b_ref, o_ref, ob_ref, *, alpha):
    y = jnp.dot(a_ref[...], w_ref[...], preferred_element_type=F32)
    out = _layer_norm(alpha * x_ref[...] + y, g_ref[...], b_ref[...])
    o_ref[...] = out
    ob_ref[...] = out.astype(BF16)


def _matmul_res_ln(a, w, x, g, b, *, alpha, name):
    m, k = a.shape
    d = w.shape[1]
    tm = _pick_tile(m, (256, 128))
    row = lambda i: (i, 0)
    const = lambda i: (0, 0)
    return pl.pallas_call(
        functools.partial(_mm_res_ln_kernel, alpha=alpha),
        out_shape=(jax.ShapeDtypeStruct((m, d), F32),
                   jax.ShapeDtypeStruct((m, d), BF16)),
        grid=(m // tm,),
        in_specs=[
            pl.BlockSpec((tm, k), row),
            pl.BlockSpec((k, d), const, pipeline_mode=pl.Buffered(1)),
            pl.BlockSpec((tm, d), row),
            pl.BlockSpec((1, d), const),
            pl.BlockSpec((1, d), const),
        ],
        out_specs=(pl.BlockSpec((tm, d), row), pl.BlockSpec((tm, d), row)),
        compiler_params=_params("parallel"),
        name=name,
    )(a, w, x, g.reshape(1, d), b.reshape(1, d))


def _res_ln_kernel(x_ref, y_ref, g_ref, b_ref, o_ref, ob_ref, *, alpha):
    out = _layer_norm(alpha * x_ref[...] + y_ref[...], g_ref[...], b_ref[...])
    o_ref[...] = out
    ob_ref[...] = out.astype(BF16)


def _res_ln(x, y, g, b, *, alpha, name):
    m, d = x.shape
    tm = _pick_tile(m, (512, 256, 128))
    row = lambda i: (i, 0)
    const = lambda i: (0, 0)
    return pl.pallas_call(
        functools.partial(_res_ln_kernel, alpha=alpha),
        out_shape=(jax.ShapeDtypeStruct((m, d), F32),
                   jax.ShapeDtypeStruct((m, d), BF16)),
        grid=(m // tm,),
        in_specs=[pl.BlockSpec((tm, d), row), pl.BlockSpec((tm, d), row),
                  pl.BlockSpec((1, d), const), pl.BlockSpec((1, d), const)],
        out_specs=(pl.BlockSpec((tm, d), row), pl.BlockSpec((tm, d), row)),
        compiler_params=_params("parallel"),
        name=name,
    )(x, y, g.reshape(1, d), b.reshape(1, d))


def _gmlp_mix_kernel(u_ref, v_ref, wmix_ref, bmix_ref, lg_ref, lb_ref,
                     prod_ref, vln_ref, *, groups, gdim):
    vln = _layer_norm(v_ref[...], lg_ref[...], lb_ref[...])
    vln_ref[...] = vln
    for g in range(groups):
        sl = slice(g * gdim, (g + 1) * gdim)
        mixed = jnp.dot(wmix_ref[0, g].astype(BF16), vln[:, sl].astype(BF16),
                        preferred_element_type=F32)
        mixed = mixed + bmix_ref[0, :, g:g + 1]
        prod_ref[:, sl] = (u_ref[:, sl].astype(F32) * mixed).astype(prod_ref.dtype)


def _gmlp_mix(u, v, wmix, bmix, lnv_g, lnv_b, *, n_prompt_chunks, n_sample_rows):
    n, half = v.shape
    groups, chunk = wmix.shape[1], wmix.shape[2]
    gdim = half // groups
    kind = lambda i: jnp.where(i >= n_prompt_chunks, 1, 0)
    row = lambda i: (i, 0)
    const = lambda i: (0, 0)
    return pl.pallas_call(
        functools.partial(_gmlp_mix_kernel, groups=groups, gdim=gdim),
        out_shape=(jax.ShapeDtypeStruct((n, half), BF16),
                   jax.ShapeDtypeStruct((n_sample_rows, half), F32)),
        grid=(n // chunk,),
        in_specs=[
            pl.BlockSpec((chunk, half), row),
            pl.BlockSpec((chunk, half), row),
            pl.BlockSpec((1, groups, chunk, chunk), lambda i: (kind(i), 0, 0, 0)),
            pl.BlockSpec((1, chunk, groups), lambda i: (kind(i), 0, 0)),
            pl.BlockSpec((1, half), const),
            pl.BlockSpec((1, half), const),
        ],
        out_specs=(
            pl.BlockSpec((chunk, half), row),
            pl.BlockSpec((chunk, half),
                         lambda i: (jnp.maximum(i - n_prompt_chunks, 0), 0)),
        ),
        compiler_params=_params("arbitrary"),
        name="gmlp_mix",
    )(u, v, wmix, bmix, lnv_g.reshape(1, half), lnv_b.reshape(1, half))


def _peer_select_kernel(xt_ref, wqt_ref, sk_ref, cnt0_ref, pz0_ref, rank1_ref, p1_ref,
                        qt_ref, s0_ref, tops_ref, *, heads, nkeys):
    qt_ref[...] = jnp.dot(wqt_ref[...], xt_ref[...], preferred_element_type=F32)
    tm = qt_ref.shape[1]
    for h in range(heads):
        for p in range(2):
            r0 = (2 * h + p) * nkeys
            qhp = qt_ref[r0:r0 + nkeys, :].astype(BF16)
            s = jnp.dot(sk_ref[h, p], qhp, preferred_element_type=F32)
            cur = jnp.full((1, tm), jnp.inf, F32)
            rank = jnp.full(s.shape, NOT_RANKED, F32)
            for r in range(PEER_TOPK):
                cur = jnp.max(jnp.where(s < cur, s, -jnp.inf), axis=0, keepdims=True)
                tops_ref[p, r, h:h + 1, :] = cur
                if p == 1:
                    rank = jnp.where(s == cur, float(r), rank)
            if p == 0:
                s0_ref[h] = s
            else:
                rank1_ref[h] = pltpu.bitcast(rank.astype(BF16), jnp.uint32)
                p1_ref[h] = pltpu.bitcast(
                    jnp.exp(s - tops_ref[1, 0, h:h + 1, :]).astype(BF16), jnp.uint32)

    a = [tops_ref[0, r] for r in range(PEER_TOPK)]
    b = [tops_ref[1, r] for r in range(PEER_TOPK)]
    cands = [a[k] + b[l] for k in range(PEER_TOPK) for l in range(PEER_TOPK)
             if (k + 1) * (l + 1) <= PEER_TOPK]
    cur = jnp.full(cands[0].shape, jnp.inf, F32)
    for _ in range(PEER_TOPK):
        nxt = jnp.full(cands[0].shape, -jnp.inf, F32)
        for c in cands:
            nxt = jnp.maximum(nxt, jnp.where(c < cur, c, -jnp.inf))
        cur = nxt
    thr = cur
    top = cands[0]
    z = jnp.zeros(thr.shape, F32)
    for c in cands:
        z = z + jnp.where(c >= thr, jnp.exp(c - top), 0.0)
    inv_z = 1.0 / z
    for h in range(heads):
        s0 = s0_ref[h]
        cnt = jnp.zeros(s0.shape, F32)
        for l in range(PEER_TOPK):
            cnt = cnt + jnp.where(s0 + b[l][h:h + 1, :] >= thr[h:h + 1, :], 1.0, 0.0)
        cnt0_ref[h] = cnt
        pz0_ref[h] = jnp.exp(s0 - a[0][h:h + 1, :]) * inv_z[h:h + 1, :]


def _peer_select(xt, wqt, sk):
    d, n = xt.shape
    heads, _, nkeys, _ = sk.shape
    tm = _pick_tile(n, (256, 128))
    key_spec = pl.BlockSpec((heads, nkeys, tm), lambda t: (0, 0, t))
    pair_spec = pl.BlockSpec((heads, nkeys // 2, tm), lambda t: (0, 0, t))
    return pl.pallas_call(
        functools.partial(_peer_select_kernel, heads=heads, nkeys=nkeys),
        out_shape=(jax.ShapeDtypeStruct((heads, nkeys, n), F32),
                   jax.ShapeDtypeStruct((heads, nkeys, n), F32),
                   jax.ShapeDtypeStruct((heads, nkeys // 2, n), jnp.uint32),
                   jax.ShapeDtypeStruct((heads, nkeys // 2, n), jnp.uint32)),
        grid=(n // tm,),
        in_specs=[
            pl.BlockSpec((d, tm), lambda t: (0, t)),
            pl.BlockSpec(wqt.shape, lambda t: (0, 0), pipeline_mode=pl.Buffered(1)),
            pl.BlockSpec(sk.shape, lambda t: (0, 0, 0, 0)),
        ],
        out_specs=(key_spec, key_spec, pair_spec, pair_spec),
        scratch_shapes=[pltpu.VMEM((wqt.shape[0], tm), F32),
                        pltpu.VMEM((heads, nkeys, tm), F32),
                        pltpu.VMEM((2, PEER_TOPK, heads, tm), F32)],
        compiler_params=_params("parallel"),
        name="peer_select",
    )(xt, wqt, sk)


def _peer_expert_kernel(xt_ref, u_ref, vt_ref, cnt0_ref, pz0_ref, rank1_ref, p1_ref,
                        out_ref, at0_ref, at1_ref, wt0_ref, wt1_ref, cnt_s, pz_s,
                        *, heads, nkeys, n_blocks, n_items):
    s = pl.program_id(0)
    eb, tm = at0_ref.shape
    rows = eb // nkeys

    @pl.when(s == 0)
    def _():
        at1_ref[...] = jnp.zeros_like(at1_ref)
        wt0_ref[...] = jnp.zeros_like(wt0_ref)

    @pl.when((s <= 2) | (lax.rem(s - 2, n_blocks) == 0))
    def _():
        out_ref[...] = jnp.zeros_like(out_ref)

    item_b = jnp.clip(s - 1, 0, n_items - 1)
    valid_b = jnp.where((s >= 1) & (s <= n_items), 1.0, 0.0).astype(F32)
    first_row = pl.multiple_of(lax.rem(item_b, n_blocks) * rows, rows)

    d_model = out_ref.shape[0]
    d_chunk = d_model // rows

    def step(at_w, at_r, wt_w, wt_r):
        for h in range(heads):
            cnt_rows = cnt0_ref[h, pl.ds(first_row, rows), :]
            pz_rows = pz0_ref[h, pl.ds(first_row, rows), :] * valid_b
            for ii in range(rows):
                cnt_s[h, ii] = pltpu.bitcast(
                    jnp.broadcast_to(cnt_rows[ii:ii + 1, :], (BF16_SUBLANES, tm)).astype(BF16),
                    jnp.uint32)
                pz_s[h, ii] = pltpu.bitcast(
                    jnp.broadcast_to(pz_rows[ii:ii + 1, :], (BF16_SUBLANES, tm)).astype(BF16),
                    jnp.uint32)
        reps = nkeys // BF16_SUBLANES
        for ii in range(rows):
            keys = slice(ii * nkeys, (ii + 1) * nkeys)
            for tc in range(tm // LANES):
                lanes = slice(tc * LANES, (tc + 1) * LANES)
                g = jnp.zeros((nkeys, LANES), BF16)
                for h in range(heads):
                    cnt = pltpu.bitcast(pltpu.repeat(cnt_s[h, ii, :, lanes], reps, axis=0), BF16)
                    pz = pltpu.bitcast(pltpu.repeat(pz_s[h, ii, :, lanes], reps, axis=0), BF16)
                    r1 = pltpu.bitcast(rank1_ref[h, :, lanes], BF16)
                    p1 = pltpu.bitcast(p1_ref[h, :, lanes], BF16)
                    g = g + jnp.where(r1 < cnt, p1 * pz, jnp.zeros_like(p1))
                wt_w[keys, lanes] = (jax.nn.gelu(at_r[keys, lanes]) * g.astype(F32)).astype(BF16)
            dd = slice(ii * d_chunk, (ii + 1) * d_chunk)
            out_ref[dd, :] += jnp.dot(vt_ref[dd, :], wt_r[...], preferred_element_type=F32)
        at_w[...] = jnp.dot(u_ref[...], xt_ref[...], preferred_element_type=F32)

    @pl.when(lax.rem(s, 2) == 0)
    def _():
        step(at0_ref, at1_ref, wt1_ref, wt0_ref)

    @pl.when(lax.rem(s, 2) == 1)
    def _():
        step(at1_ref, at0_ref, wt0_ref, wt1_ref)


def _peer_expert(xt, u, vt, cnt0, pz0, rank1, p1, *, layer):
    d, n = xt.shape
    n_exp = u.shape[1]
    heads, nkeys, _ = cnt0.shape
    tm = _pick_tile(n, (512, 256, 128))
    eb = F32_SUBLANES * nkeys
    n_blocks = n_exp // eb
    n_items = (n // tm) * n_blocks

    def item(s, lag):
        return jnp.clip(s - lag, 0, n_items - 1)

    tile_b = lambda s: item(s, 1) // n_blocks
    once = pl.Buffered(1)
    row_spec = pl.BlockSpec((heads, nkeys, tm), lambda s: (0, 0, tile_b(s)), pipeline_mode=once)
    pair_spec = pl.BlockSpec((heads, nkeys // 2, tm), lambda s: (0, 0, tile_b(s)),
                             pipeline_mode=once)
    return pl.pallas_call(
        functools.partial(_peer_expert_kernel, heads=heads, nkeys=nkeys,
                          n_blocks=n_blocks, n_items=n_items),
        out_shape=jax.ShapeDtypeStruct((d, n), F32),
        grid=(n_items + 2,),
        in_specs=[
            pl.BlockSpec((d, tm), lambda s: (0, item(s, 0) // n_blocks), pipeline_mode=once),
            pl.BlockSpec((None, eb, d), lambda s: (layer, item(s, 0) % n_blocks, 0)),
            pl.BlockSpec((None, d, eb), lambda s: (layer, 0, item(s, 2) % n_blocks)),
            row_spec, row_spec, pair_spec, pair_spec,
        ],
        out_specs=pl.BlockSpec((d, tm), lambda s: (0, item(s, 2) // n_blocks)),
        scratch_shapes=[pltpu.VMEM((eb, tm), F32), pltpu.VMEM((eb, tm), F32),
                        pltpu.VMEM((eb, tm), BF16), pltpu.VMEM((eb, tm), BF16),
                        pltpu.VMEM((heads, F32_SUBLANES, F32_SUBLANES, tm), jnp.uint32),
                        pltpu.VMEM((heads, F32_SUBLANES, F32_SUBLANES, tm), jnp.uint32)],
        compiler_params=_params("arbitrary"),
        name="peer_expert",
    )(xt, u, vt, cnt0, pz0, rank1, p1)


def _peer(x, xb, w_q, sub_keys, exp_u_all, exp_vt_all, ln_g, ln_b, *, alpha, layer):
    xt = xb.T
    wqt = w_q.T.astype(BF16)
    sk = sub_keys.astype(BF16)
    cnt0, pz0, rank1, p1 = _peer_select(xt, wqt, sk)
    out_t = _peer_expert(xt, exp_u_all, exp_vt_all, cnt0, pz0, rank1, p1, layer=layer)
    return _res_ln(x, out_t.T, ln_g, ln_b, alpha=alpha, name=f"peer_ln_{layer}")


def kernel(x_prompt, x_sample, state_mlstm_C, state_mlstm_n, state_mlstm_m,
           w_in_a, b_gate_a, hn_gain_a, w_out_a, w_in_b, b_in_b, lnv_g_b, lnv_b_b,
           w_s_b, b_s_b, w_out_b, ln_mix_g, ln_mix_b, ln_ffn_g, ln_ffn_b,
           peer_w_q, peer_sub_keys, peer_u, peer_v):
    bp, tp, d = x_prompt.shape
    bs, ts, _ = x_sample.shape
    depth = ln_mix_g.shape[0]
    alpha = float((2 * depth) ** 0.25)
    heads = b_gate_a.shape[-1] // 2
    dv = hn_gain_a.shape[-1]
    dk = state_mlstm_n.shape[-1]
    hk, hv = heads * dk, heads * dv
    n_prompt, n_sample = bp * tp, bs * ts
    assert depth == 2 and w_in_a.shape[0] == 1 and w_in_b.shape[0] == 1
    assert ts <= SAMPLE_PAD_LEN and tp % MLSTM_CHUNK == 0

    x = jnp.concatenate([x_prompt.reshape(n_prompt, d), x_sample.reshape(n_sample, d)])
    xb = x.astype(BF16)

    n_main = 2 * hk + 2 * hv
    proj = _matmul(xb, w_in_a[0][:, :n_main].astype(BF16), jnp.zeros((n_main,), F32),
                   name="mlstm_in_proj")
    w_gate = jnp.pad(w_in_a[0][:, n_main:], ((0, 0), (0, GATE_LANES - 2 * heads)))
    b_gate = jnp.pad(b_gate_a[0], (0, GATE_LANES - 2 * heads)).reshape(1, GATE_LANES)
    gates = _gates(x, w_gate, b_gate)

    zeros_c = jnp.zeros((bp, heads, dk, dv), F32)
    hn_p, c_p, n_p, m_p = _mlstm(
        proj[None], gates[None],
        zeros_c, jnp.zeros((bp, heads, dk), F32), jnp.zeros((bp, heads), F32),
        hn_gain_a[0], bsz=bp, seq_len=tp, heads=heads, dk=dk, dv=dv, chunk=MLSTM_CHUNK,
        name="mlstm_prompt")

    pad_t = SAMPLE_PAD_LEN - ts
    proj_s = jnp.pad(proj[n_prompt:].reshape(bs, ts, n_main), ((0, 0), (0, pad_t), (0, 0)))
    lane = jnp.arange(GATE_LANES)
    pad_gate = jnp.where(lane < heads, NEG_BIG, jnp.where(lane < 2 * heads, POS_BIG, 0.0))
    gates_s = jnp.concatenate(
        [gates[n_prompt:].reshape(bs, ts, GATE_LANES),
         jnp.broadcast_to(pad_gate.astype(F32), (bs, pad_t, GATE_LANES))], axis=1)
    hn_s, c_s, n_s, m_s = _mlstm(
        proj_s, gates_s, state_mlstm_C[0], state_mlstm_n[0], state_mlstm_m[0],
        hn_gain_a[0], bsz=bs, seq_len=SAMPLE_PAD_LEN, heads=heads, dk=dk, dv=dv,
        chunk=SAMPLE_PAD_LEN, name="mlstm_sample")

    hn = jnp.concatenate([hn_p[0], hn_s[:, :ts].reshape(n_sample, hv)])
    x, xb = _matmul_res_ln(hn, w_out_a[0].astype(BF16), x, ln_mix_g[0], ln_mix_b[0],
                           alpha=alpha, name="mlstm_out_proj_ln")
    exp_u_all = peer_u.astype(BF16)
    exp_vt_all = jnp.swapaxes(peer_v, 1, 2).astype(BF16)
    x, xb = _peer(x, xb, peer_w_q[0], peer_sub_keys[0], exp_u_all, exp_vt_all,
                  ln_ffn_g[0], ln_ffn_b[0], alpha=alpha, layer=0)

    half = w_in_b.shape[-1] // 2
    groups, chunk = w_s_b.shape[1], w_s_b.shape[2]
    assert tp % chunk == 0 and n_sample % chunk == 0 and chunk % ts == 0
    w_in = w_in_b[0].astype(BF16)
    u = _matmul(xb, w_in[:, :half], b_in_b[0][:half], act="gelu", out_dtype=BF16,
                name="gmlp_in_proj_u")
    v = _matmul(xb, w_in[:, half:], b_in_b[0][half:], act="gelu", name="gmlp_in_proj_v")
    ws = jnp.where(jnp.tril(jnp.ones((chunk, chunk), dtype=bool)), w_s_b[0], 0.0)
    ws_sample = jnp.einsum("ab,gts->gatbs", jnp.eye(chunk // ts, dtype=F32),
                           ws[:, :ts, :ts]).reshape(groups, chunk, chunk)
    wmix = jnp.stack([ws, ws_sample])
    bmix = jnp.stack([b_s_b[0].T, jnp.tile(b_s_b[0][:, :ts].T, (chunk // ts, 1))])
    prod, v_s = _gmlp_mix(u, v, wmix, bmix, lnv_g_b[0], lnv_b_b[0],
                          n_prompt_chunks=n_prompt // chunk, n_sample_rows=n_sample)
    x, xb = _matmul_res_ln(prod, w_out_b[0].astype(BF16), x, ln_mix_g[1], ln_mix_b[1],
                           alpha=alpha, name="gmlp_out_proj_ln")
    x, xb = _peer(x, xb, peer_w_q[1], peer_sub_keys[1], exp_u_all, exp_vt_all,
                  ln_ffn_g[1], ln_ffn_b[1], alpha=alpha, layer=1)

    return (x[:n_prompt].reshape(bp, tp, d), x[n_prompt:].reshape(bs, ts, d),
            c_p[None], n_p[None], m_p[None], c_s[None], n_s[None], m_s[None],
            v_s.reshape(1, bs, ts, half))
```

```python
import functools
import math

import jax
import jax.numpy as jnp
from jax import lax
from jax.experimental import pallas as pl
from jax.experimental.pallas import tpu as pltpu

F32 = jnp.float32
BF16 = jnp.bfloat16

LN_EPS = 1e-5
PEER_TOPK = 16
MLSTM_CHUNK = 256
SAMPLE_PAD_LEN = 8
MLSTM_STAGE_MAJOR_MAX_CHUNK = 256
GATE_LANES = 128
LANES = 128
F32_SUBLANES = 8
BF16_SUBLANES = 16
NOT_RANKED = 255.0
V7X_VMEM_LIMIT_BYTES = 48 * 1024 * 1024
PEER_EXPERT_VMEM_LIMIT_BYTES = 56 * 1024 * 1024
PEER_EXPERT_ROWS = 8
NEG_BIG = -1e30
POS_BIG = 1e30


def _params(*sem):
    return pltpu.CompilerParams(dimension_semantics=sem,
                                vmem_limit_bytes=V7X_VMEM_LIMIT_BYTES)


def _pick_tile(n, candidates):
    for c in candidates:
        if n % c == 0:
            return c
    raise ValueError(f"no tile in {candidates} divides {n}")


def _layer_norm(z, g, b):
    mu = jnp.mean(z, axis=-1, keepdims=True)
    zc = z - mu
    var = jnp.mean(zc * zc, axis=-1, keepdims=True)
    return zc * lax.rsqrt(var + LN_EPS) * g + b


def _gelu_tanh(x):
    k0 = -2.0 * math.sqrt(2.0 / math.pi) * math.log2(math.e)
    k1 = 0.044715 * k0
    return x / (1.0 + jnp.exp2(x * (x * x * k1 + k0)))


def _log_sigmoid(x):
    return jnp.minimum(x, 0.0) - jnp.log1p(jnp.exp(-jnp.abs(x)))


def _mm_kernel(x_ref, w_ref, b_ref, o_ref, *, act):
    acc = jnp.dot(x_ref[...], w_ref[...], preferred_element_type=F32)
    acc = acc + b_ref[...]
    if act == "gelu":
        acc = _gelu_tanh(acc)
    o_ref[...] = acc.astype(o_ref.dtype)


def _matmul(x, w, bias, *, act=None, out_dtype=F32, name):
    m, k = x.shape
    n = w.shape[1]
    tm = _pick_tile(m, (512, 256, 128))
    tn = _pick_tile(n, (1024, 512, 256, 128))
    return pl.pallas_call(
        functools.partial(_mm_kernel, act=act),
        out_shape=jax.ShapeDtypeStruct((m, n), out_dtype),
        grid=(n // tn, m // tm),
        in_specs=[
            pl.BlockSpec((tm, k), lambda j, i: (i, 0)),
            pl.BlockSpec((k, tn), lambda j, i: (0, j)),
            pl.BlockSpec((1, tn), lambda j, i: (0, j)),
        ],
        out_specs=pl.BlockSpec((tm, tn), lambda j, i: (i, j)),
        compiler_params=_params("parallel", "parallel"),
        name=name,
    )(x, w, bias.reshape(1, n).astype(F32))


def _gate_kernel(x_ref, w_ref, b_ref, o_ref):
    x = x_ref[...]
    w = w_ref[...]
    xh = x.astype(BF16)
    xl = (x - xh.astype(F32)).astype(BF16)
    wh = w.astype(BF16)
    wl = (w - wh.astype(F32)).astype(BF16)
    acc = jnp.dot(xh, wh, preferred_element_type=F32)
    acc = acc + jnp.dot(xl, wh, preferred_element_type=F32)
    acc = acc + jnp.dot(xh, wl, preferred_element_type=F32)
    o_ref[...] = acc + b_ref[...]


def _gates(x, w_gate, b_gate):
    m, k = x.shape
    tm = _pick_tile(m, (512, 256, 128))
    return pl.pallas_call(
        _gate_kernel,
        out_shape=jax.ShapeDtypeStruct((m, GATE_LANES), F32),
        grid=(m // tm,),
        in_specs=[
            pl.BlockSpec((tm, k), lambda i: (i, 0)),
            pl.BlockSpec((k, GATE_LANES), lambda i: (0, 0)),
            pl.BlockSpec((1, GATE_LANES), lambda i: (0, 0)),
        ],
        out_specs=pl.BlockSpec((tm, GATE_LANES), lambda i: (i, 0)),
        compiler_params=_params("parallel"),
        name="mlstm_gates",
    )(x, w_gate, b_gate)


def _mlstm_kernel(q_ref, k_ref, v_ref, o_ref, g_ref, gt_ref, c0_ref, n0_ref,
                  m0_ref, gain_ref, hn_ref, c_ref, n_ref, m_ref,
                  *, heads, dk, dv, chunk, stage_major):
    @pl.when(pl.program_id(1) == 0)
    def _():
        c_ref[...] = c0_ref[...]
        n_ref[...] = n0_ref[...]
        m_ref[...] = m0_ref[...]

    g = g_ref[0]
    gt = gt_ref[0]
    row = lax.broadcasted_iota(jnp.int32, (chunk, chunk), 0)
    col = lax.broadcasted_iota(jnp.int32, (chunk, chunk), 1)
    causal = row >= col
    tril = causal.astype(F32)
    triu = (row <= col).astype(F32)
    lf_col = _log_sigmoid(g)
    lf_row = _log_sigmoid(gt)
    b_col_all = jnp.dot(tril, lf_col, precision=lax.Precision.HIGHEST,
                        preferred_element_type=F32)
    b_row_all = jnp.dot(lf_row, triu, precision=lax.Precision.HIGHEST,
                        preferred_element_type=F32)
    k_scale = dk ** -0.5

    def gate_terms(h):
        bc = b_col_all[:, heads + h:heads + h + 1]
        br = b_row_all[heads + h:heads + h + 1, :]
        ic = g[:, h:h + 1]
        ir = gt[h:h + 1, :]
        m_prev = m_ref[0, h][:, 0:1]
        dlog = jnp.where(causal, bc - br + ir, -jnp.inf)
        inter = bc + m_prev
        m_t = jnp.maximum(inter, jnp.max(dlog, axis=-1, keepdims=True))
        b_end = bc[chunk - 1:chunk, :]
        m_new = jnp.maximum(b_end + m_prev,
                            jnp.max(b_end - br + ir, axis=-1, keepdims=True))
        return dict(decay_mat=jnp.exp(dlog - m_t), w_inter=jnp.exp(inter - m_t),
                    floor=jnp.exp(-m_t), m_new=m_new,
                    w_k=jnp.exp(b_end - bc + ic - m_new),
                    decay=jnp.exp(b_end + m_prev - m_new))

    def load_qkv(h):
        qh = q_ref[0, :, h * dk:(h + 1) * dk]
        kh = k_ref[0, :, h * dk:(h + 1) * dk] * k_scale
        vb = v_ref[0, :, h * dv:(h + 1) * dv].astype(BF16)
        return qh, kh, vb

    def scores(qh, kh):
        return lax.dot_general(qh.astype(BF16), kh.astype(BF16), (((1,), (1,)), ((), ())),
                               preferred_element_type=F32)

    def read_out(h, qh, vb, qk, t):
        c_prev = c_ref[0, h]
        n_prev = n_ref[0, h]
        s = qk * t["decay_mat"]
        num = jnp.dot(s.astype(BF16), vb, preferred_element_type=F32)
        num = num + t["w_inter"] * jnp.dot(qh.astype(BF16), c_prev.astype(BF16),
                                           preferred_element_type=F32)
        den = jnp.sum(s, axis=-1, keepdims=True)
        den = den + t["w_inter"] * jnp.sum(qh * n_prev, axis=-1, keepdims=True)
        return num / jnp.maximum(jnp.abs(den), t["floor"])

    def gate_and_norm(h, hval):
        hg = hval * jax.nn.sigmoid(o_ref[0, :, h * dv:(h + 1) * dv])
        mu = jnp.mean(hg, axis=-1, keepdims=True)
        hc = hg - mu
        var = jnp.mean(hc * hc, axis=-1, keepdims=True)
        hn = hc * lax.rsqrt(var + LN_EPS) * gain_ref[h]
        hn_ref[0, :, h * dv:(h + 1) * dv] = hn.astype(hn_ref.dtype)

    def update_state(h, kh, vb, t):
        kw = kh * t["w_k"]
        c_ref[0, h] = t["decay"] * c_ref[0, h] + lax.dot_general(
            kw.astype(BF16), vb, (((0,), (0,)), ((), ())), preferred_element_type=F32)
        n_ref[0, h] = t["decay"] * n_ref[0, h] + jnp.sum(kw, axis=0, keepdims=True)
        m_ref[0, h] = jnp.broadcast_to(t["m_new"], (1, GATE_LANES))

    if stage_major:
        qkv = [load_qkv(h) for h in range(heads)]
        qks = [scores(qh, kh) for qh, kh, _ in qkv]
        terms = [gate_terms(h) for h in range(heads)]
        hvals = [read_out(h, qkv[h][0], qkv[h][2], qks[h], terms[h]) for h in range(heads)]
        for h in range(heads):
            update_state(h, qkv[h][1], qkv[h][2], terms[h])
        for h in range(heads):
            gate_and_norm(h, hvals[h])
    else:
        for h in range(heads):
            qh, kh, vb = load_qkv(h)
            t = gate_terms(h)
            gate_and_norm(h, read_out(h, qh, vb, scores(qh, kh), t))
            update_state(h, kh, vb, t)


def _mlstm(proj, gates, c0, n0, m0, gain, *, bsz, seq_len, heads, dk, dv, chunk, name):
    hk, hv = heads * dk, heads * dv
    chunks = seq_len // chunk
    if proj.shape[0] == 1:
        rows = lambda b, c: (0, b * chunks + c)
    else:
        rows = lambda b, c: (b, c)
    gt = jnp.swapaxes(gates[..., :2 * heads], 1, 2)
    n0 = n0.reshape(bsz, heads, 1, dk)
    m0 = jnp.broadcast_to(m0[..., None, None], (bsz, heads, 1, GATE_LANES))
    state_spec = lambda shape: pl.BlockSpec((1,) + shape, lambda b, c: (b, 0, 0, 0))
    tok_spec = lambda width, col: pl.BlockSpec((1, chunk, width), lambda b, c: rows(b, c) + (col,))
    hn, c, n, m = pl.pallas_call(
        functools.partial(_mlstm_kernel, heads=heads, dk=dk, dv=dv, chunk=chunk,
                          stage_major=chunk <= MLSTM_STAGE_MAJOR_MAX_CHUNK),
        out_shape=(
            jax.ShapeDtypeStruct((proj.shape[0], bsz * seq_len // proj.shape[0], hv), BF16),
            jax.ShapeDtypeStruct((bsz, heads, dk, dv), F32),
            jax.ShapeDtypeStruct((bsz, heads, 1, dk), F32),
            jax.ShapeDtypeStruct((bsz, heads, 1, GATE_LANES), F32),
        ),
        grid=(bsz, chunks),
        in_specs=[
            tok_spec(hk, 0),
            tok_spec(hk, 1),
            tok_spec(hv, 1),
            tok_spec(hv, 2),
            tok_spec(GATE_LANES, 0),
            pl.BlockSpec((1, 2 * heads, chunk),
                         lambda b, c: (rows(b, c)[0], 0, rows(b, c)[1])),
            state_spec((heads, dk, dv)),
            state_spec((heads, 1, dk)),
            state_spec((heads, 1, GATE_LANES)),
            pl.BlockSpec((heads, 1, dv), lambda b, c: (0, 0, 0)),
        ],
        out_specs=(
            tok_spec(hv, 0),
            state_spec((heads, dk, dv)),
            state_spec((heads, 1, dk)),
            state_spec((heads, 1, GATE_LANES)),
        ),
        compiler_params=_params("parallel", "arbitrary"),
        name=name,
    )(proj, proj, proj, proj, gates, gt, c0, n0, m0, gain.reshape(heads, 1, dv))
    return hn, c, n.reshape(bsz, heads, dk), m[:, :, 0, 0]


def _mm_res_ln_kernel(a_ref, w_ref, x_ref, g_ref, b_ref, o_ref, obt_ref, *, alpha):
    y = jnp.dot(a_ref[...], w_ref[...], preferred_element_type=F32)
    out = _layer_norm(alpha * x_ref[...] + y, g_ref[...], b_ref[...])
    o_ref[...] = out
    obt_ref[...] = out.T.astype(BF16)


def _matmul_res_ln(a, w, x, g, b, *, alpha, name):
    m, k = a.shape
    d = w.shape[1]
    tm = _pick_tile(m, (256, 128))
    row = lambda i: (i, 0)
    const = lambda i: (0, 0)
    return pl.pallas_call(
        functools.partial(_mm_res_ln_kernel, alpha=alpha),
        out_shape=(jax.ShapeDtypeStruct((m, d), F32),
                   jax.ShapeDtypeStruct((d, m), BF16)),
        grid=(m // tm,),
        in_specs=[
            pl.BlockSpec((tm, k), row),
            pl.BlockSpec((k, d), const, pipeline_mode=pl.Buffered(1)),
            pl.BlockSpec((tm, d), row),
            pl.BlockSpec((1, d), const),
            pl.BlockSpec((1, d), const),
        ],
        out_specs=(pl.BlockSpec((tm, d), row), pl.BlockSpec((d, tm), lambda i: (0, i))),
        compiler_params=_params("parallel"),
        name=name,
    )(a, w, x, g.reshape(1, d), b.reshape(1, d))


def _res_ln_kernel(x_ref, yt_ref, g_ref, b_ref, o_ref, ob_ref, *, alpha):
    out = _layer_norm(alpha * x_ref[...] + yt_ref[...].T, g_ref[...], b_ref[...])
    o_ref[...] = out
    ob_ref[...] = out.astype(BF16)


def _res_ln(x, yt, g, b, *, alpha, name):
    m, d = x.shape
    tm = _pick_tile(m, (256, 128))
    row = lambda i: (i, 0)
    const = lambda i: (0, 0)
    return pl.pallas_call(
        functools.partial(_res_ln_kernel, alpha=alpha),
        out_shape=(jax.ShapeDtypeStruct((m, d), F32),
                   jax.ShapeDtypeStruct((m, d), BF16)),
        grid=(m // tm,),
        in_specs=[pl.BlockSpec((tm, d), row), pl.BlockSpec((d, tm), lambda i: (0, i)),
                  pl.BlockSpec((1, d), const), pl.BlockSpec((1, d), const)],
        out_specs=(pl.BlockSpec((tm, d), row), pl.BlockSpec((tm, d), row)),
        compiler_params=_params("parallel"),
        name=name,
    )(x, yt, g.reshape(1, d), b.reshape(1, d))


def _gmlp_mix_kernel(u_ref, v_ref, wmix_ref, bmix_ref, lg_ref, lb_ref,
                     prod_ref, vln_ref, *, groups, gdim):
    vln = _layer_norm(v_ref[...], lg_ref[...], lb_ref[...])
    vln_ref[...] = vln
    for g in range(groups):
        sl = slice(g * gdim, (g + 1) * gdim)
        mixed = jnp.dot(wmix_ref[0, g].astype(BF16), vln[:, sl].astype(BF16),
                        preferred_element_type=F32)
        mixed = mixed + bmix_ref[0, :, g:g + 1]
        prod_ref[:, sl] = (u_ref[:, sl].astype(F32) * mixed).astype(prod_ref.dtype)


def _gmlp_mix(u, v, wmix, bmix, lnv_g, lnv_b, *, n_prompt_chunks, n_sample_rows):
    n, half = v.shape
    groups, chunk = wmix.shape[1], wmix.shape[2]
    gdim = half // groups
    kind = lambda i: jnp.where(i >= n_prompt_chunks, 1, 0)
    row = lambda i: (i, 0)
    const = lambda i: (0, 0)
    return pl.pallas_call(
        functools.partial(_gmlp_mix_kernel, groups=groups, gdim=gdim),
        out_shape=(jax.ShapeDtypeStruct((n, half), BF16),
                   jax.ShapeDtypeStruct((n_sample_rows, half), F32)),
        grid=(n // chunk,),
        in_specs=[
            pl.BlockSpec((chunk, half), row),
            pl.BlockSpec((chunk, half), row),
            pl.BlockSpec((1, groups, chunk, chunk), lambda i: (kind(i), 0, 0, 0)),
            pl.BlockSpec((1, chunk, groups), lambda i: (kind(i), 0, 0)),
            pl.BlockSpec((1, half), const),
            pl.BlockSpec((1, half), const),
        ],
        out_specs=(
            pl.BlockSpec((chunk, half), row),
            pl.BlockSpec((chunk, half),
                         lambda i: (jnp.maximum(i - n_prompt_chunks, 0), 0)),
        ),
        compiler_params=_params("arbitrary"),
        name="gmlp_mix",
    )(u, v, wmix, bmix, lnv_g.reshape(1, half), lnv_b.reshape(1, half))


def _peer_select_kernel(xt_ref, wqt_ref, sk_ref, cnt0_ref, pz0_ref, rank1_ref, p1_ref,
                        qt_ref, s0_ref, tops_ref, *, heads, nkeys):
    qt_ref[...] = jnp.dot(wqt_ref[...], xt_ref[...], preferred_element_type=F32)
    tm = qt_ref.shape[1]
    for h in range(heads):
        for p in range(2):
            r0 = (2 * h + p) * nkeys
            qhp = qt_ref[r0:r0 + nkeys, :].astype(BF16)
            s = jnp.dot(sk_ref[h, p], qhp, preferred_element_type=F32)
            cur = jnp.full((1, tm), jnp.inf, F32)
            rank = jnp.full(s.shape, NOT_RANKED, F32)
            for r in range(PEER_TOPK):
                cur = jnp.max(jnp.where(s < cur, s, -jnp.inf), axis=0, keepdims=True)
                tops_ref[p, r, h:h + 1, :] = cur
                if p == 1:
                    rank = jnp.where(s == cur, float(r), rank)
            if p == 0:
                s0_ref[h] = s
            else:
                rank1_ref[h] = pltpu.bitcast(rank.astype(BF16), jnp.uint32)
                p1_ref[h] = pltpu.bitcast(
                    jnp.exp(s - tops_ref[1, 0, h:h + 1, :]).astype(BF16), jnp.uint32)

    a = [tops_ref[0, r] for r in range(PEER_TOPK)]
    b = [tops_ref[1, r] for r in range(PEER_TOPK)]
    cands = [a[k] + b[l] for k in range(PEER_TOPK) for l in range(PEER_TOPK)
             if (k + 1) * (l + 1) <= PEER_TOPK]
    cur = jnp.full(cands[0].shape, jnp.inf, F32)
    for _ in range(PEER_TOPK):
        nxt = jnp.full(cands[0].shape, -jnp.inf, F32)
        for c in cands:
            nxt = jnp.maximum(nxt, jnp.where(c < cur, c, -jnp.inf))
        cur = nxt
    thr = cur
    top = cands[0]
    z = jnp.zeros(thr.shape, F32)
    for c in cands:
        z = z + jnp.where(c >= thr, jnp.exp(c - top), 0.0)
    inv_z = 1.0 / z
    for h in range(heads):
        s0 = s0_ref[h]
        thr_h = thr[h:h + 1, :]
        b_h = [b[l][h:h + 1, :] for l in range(PEER_TOPK)]
        cnt = jnp.zeros(s0.shape, F32)
        step = PEER_TOPK // 2
        while step >= 1:
            probe = b_h[step - 1]
            for base in range(2 * step, PEER_TOPK, 2 * step):
                probe = jnp.where(cnt == float(base), b_h[base + step - 1], probe)
            cnt = cnt + jnp.where(s0 + probe >= thr_h, float(step), 0.0)
            step //= 2
        cnt = jnp.where(s0 + b_h[PEER_TOPK - 1] >= thr_h, float(PEER_TOPK), cnt)
        cnt0_ref[h] = cnt
        pz0_ref[h] = jnp.exp(s0 - a[0][h:h + 1, :]) * inv_z[h:h + 1, :]


def _peer_select(xt, wqt, sk):
    d, n = xt.shape
    heads, _, nkeys, _ = sk.shape
    tm = _pick_tile(n, (256, 128))
    key_spec = pl.BlockSpec((heads, nkeys, tm), lambda t: (0, 0, t))
    pair_spec = pl.BlockSpec((heads, nkeys // 2, tm), lambda t: (0, 0, t))
    return pl.pallas_call(
        functools.partial(_peer_select_kernel, heads=heads, nkeys=nkeys),
        out_shape=(jax.ShapeDtypeStruct((heads, nkeys, n), F32),
                   jax.ShapeDtypeStruct((heads, nkeys, n), F32),
                   jax.ShapeDtypeStruct((heads, nkeys // 2, n), jnp.uint32),
                   jax.ShapeDtypeStruct((heads, nkeys // 2, n), jnp.uint32)),
        grid=(n // tm,),
        in_specs=[
            pl.BlockSpec((d, tm), lambda t: (0, t)),
            pl.BlockSpec(wqt.shape, lambda t: (0, 0), pipeline_mode=pl.Buffered(1)),
            pl.BlockSpec(sk.shape, lambda t: (0, 0, 0, 0)),
        ],
        out_specs=(key_spec, key_spec, pair_spec, pair_spec),
        scratch_shapes=[pltpu.VMEM((wqt.shape[0], tm), F32),
                        pltpu.VMEM((heads, nkeys, tm), F32),
                        pltpu.VMEM((2, PEER_TOPK, heads, tm), F32)],
        compiler_params=_params("parallel"),
        name="peer_select",
    )(xt, wqt, sk)


def _peer_expert_kernel(xt_ref, u_ref, vt_ref, cnt0_ref, pz0_ref, rank1_ref, p1_ref,
                        out_ref, at0_ref, at1_ref, wt0_ref, wt1_ref, cnt_s, pz_s,
                        *, heads, nkeys, n_blocks, n_items):
    s = pl.program_id(0)
    eb, tm = at0_ref.shape
    rows = eb // nkeys

    @pl.when(s == 0)
    def _():
        at1_ref[...] = jnp.zeros_like(at1_ref)
        wt0_ref[...] = jnp.zeros_like(wt0_ref)

    @pl.when((s <= 2) | (lax.rem(s - 2, n_blocks) == 0))
    def _():
        out_ref[...] = jnp.zeros_like(out_ref)

    item_b = jnp.clip(s - 1, 0, n_items - 1)
    valid_b = jnp.where((s >= 1) & (s <= n_items), 1.0, 0.0).astype(F32)
    first_row = pl.multiple_of(lax.rem(item_b, n_blocks) * rows, rows)

    d_model = out_ref.shape[0]
    d_chunk = d_model // rows

    def step(at_w, at_r, wt_w, wt_r):
        for h in range(heads):
            cnt_rows = cnt0_ref[h, pl.ds(first_row, rows), :]
            pz_rows = pz0_ref[h, pl.ds(first_row, rows), :] * valid_b
            for ii in range(rows):
                cnt_s[h, ii] = pltpu.bitcast(
                    jnp.broadcast_to(cnt_rows[ii:ii + 1, :], (BF16_SUBLANES, tm)).astype(BF16),
                    jnp.uint32)
                pz_s[h, ii] = pltpu.bitcast(
                    jnp.broadcast_to(pz_rows[ii:ii + 1, :], (BF16_SUBLANES, tm)).astype(BF16),
                    jnp.uint32)
        reps = nkeys // BF16_SUBLANES
        for ii in range(rows):
            keys = slice(ii * nkeys, (ii + 1) * nkeys)
            for tc in range(tm // LANES):
                lanes = slice(tc * LANES, (tc + 1) * LANES)
                g = jnp.zeros((nkeys, LANES), BF16)
                for h in range(heads):
                    cnt = pltpu.bitcast(pltpu.repeat(cnt_s[h, ii, :, lanes], reps, axis=0), BF16)
                    pz = pltpu.bitcast(pltpu.repeat(pz_s[h, ii, :, lanes], reps, axis=0), BF16)
                    r1 = pltpu.bitcast(rank1_ref[h, :, lanes], BF16)
                    p1 = pltpu.bitcast(p1_ref[h, :, lanes], BF16)
                    g = g + jnp.where(r1 < cnt, p1 * pz, jnp.zeros_like(p1))
                wt_w[keys, lanes] = (_gelu_tanh(at_r[keys, lanes]) * g.astype(F32)).astype(BF16)
            dd = slice(ii * d_chunk, (ii + 1) * d_chunk)
            out_ref[dd, :] += jnp.dot(vt_ref[dd, :], wt_r[...], preferred_element_type=F32)
        at_w[...] = jnp.dot(u_ref[...], xt_ref[...], preferred_element_type=F32)

    @pl.when(lax.rem(s, 2) == 0)
    def _():
        step(at0_ref, at1_ref, wt1_ref, wt0_ref)

    @pl.when(lax.rem(s, 2) == 1)
    def _():
        step(at1_ref, at0_ref, wt0_ref, wt1_ref)


def _peer_expert(xt, u, vt, cnt0, pz0, rank1, p1, *, layer):
    d, n = xt.shape
    n_exp = u.shape[1]
    heads, nkeys, _ = cnt0.shape
    tm = _pick_tile(n, (512, 256, 128))
    eb = PEER_EXPERT_ROWS * nkeys
    n_blocks = n_exp // eb
    n_items = (n // tm) * n_blocks

    def item(s, lag):
        return jnp.clip(s - lag, 0, n_items - 1)

    tile_b = lambda s: item(s, 1) // n_blocks
    row_spec = pl.BlockSpec((heads, nkeys, tm), lambda s: (0, 0, tile_b(s)))
    pair_spec = pl.BlockSpec((heads, nkeys // 2, tm), lambda s: (0, 0, tile_b(s)))
    return pl.pallas_call(
        functools.partial(_peer_expert_kernel, heads=heads, nkeys=nkeys,
                          n_blocks=n_blocks, n_items=n_items),
        out_shape=jax.ShapeDtypeStruct((d, n), F32),
        grid=(n_items + 2,),
        in_specs=[
            pl.BlockSpec((d, tm), lambda s: (0, item(s, 0) // n_blocks)),
            pl.BlockSpec((None, eb, d), lambda s: (layer, item(s, 0) % n_blocks, 0)),
            pl.BlockSpec((None, None, d, eb), lambda s: (layer, item(s, 2) % n_blocks, 0, 0)),
            row_spec, row_spec, pair_spec, pair_spec,
        ],
        out_specs=pl.BlockSpec((d, tm), lambda s: (0, item(s, 2) // n_blocks)),
        scratch_shapes=[pltpu.VMEM((eb, tm), F32), pltpu.VMEM((eb, tm), F32),
                        pltpu.VMEM((eb, tm), BF16), pltpu.VMEM((eb, tm), BF16),
                        pltpu.VMEM((heads, PEER_EXPERT_ROWS, F32_SUBLANES, tm), jnp.uint32),
                        pltpu.VMEM((heads, PEER_EXPERT_ROWS, F32_SUBLANES, tm), jnp.uint32)],
        compiler_params=pltpu.CompilerParams(
            dimension_semantics=("arbitrary",), vmem_limit_bytes=PEER_EXPERT_VMEM_LIMIT_BYTES),
        name="peer_expert",
    )(xt, u, vt, cnt0, pz0, rank1, p1)


def _peer(x, xt, w_q, sub_keys, exp_u_all, exp_vt_all, ln_g, ln_b, *, alpha, layer):
    wqt = w_q.T.astype(BF16)
    sk = sub_keys.astype(BF16)
    cnt0, pz0, rank1, p1 = _peer_select(xt, wqt, sk)
    out_t = _peer_expert(xt, exp_u_all, exp_vt_all, cnt0, pz0, rank1, p1, layer=layer)
    return _res_ln(x, out_t, ln_g, ln_b, alpha=alpha, name=f"peer_ln_{layer}")


def kernel(x_prompt, x_sample, state_mlstm_C, state_mlstm_n, state_mlstm_m,
           w_in_a, b_gate_a, hn_gain_a, w_out_a, w_in_b, b_in_b, lnv_g_b, lnv_b_b,
           w_s_b, b_s_b, w_out_b, ln_mix_g, ln_mix_b, ln_ffn_g, ln_ffn_b,
           peer_w_q, peer_sub_keys, peer_u, peer_v):
    bp, tp, d = x_prompt.shape
    bs, ts, _ = x_sample.shape
    depth = ln_mix_g.shape[0]
    alpha = float((2 * depth) ** 0.25)
    heads = b_gate_a.shape[-1] // 2
    dv = hn_gain_a.shape[-1]
    dk = state_mlstm_n.shape[-1]
    hk, hv = heads * dk, heads * dv
    n_prompt, n_sample = bp * tp, bs * ts
    assert depth == 2 and w_in_a.shape[0] == 1 and w_in_b.shape[0] == 1
    assert ts <= SAMPLE_PAD_LEN and tp % MLSTM_CHUNK == 0

    x = jnp.concatenate([x_prompt.reshape(n_prompt, d), x_sample.reshape(n_sample, d)])
    xb = x.astype(BF16)

    n_main = 2 * hk + 2 * hv
    proj = _matmul(xb, w_in_a[0][:, :n_main].astype(BF16), jnp.zeros((n_main,), F32),
                   name="mlstm_in_proj")
    w_gate = jnp.pad(w_in_a[0][:, n_main:], ((0, 0), (0, GATE_LANES - 2 * heads)))
    b_gate = jnp.pad(b_gate_a[0], (0, GATE_LANES - 2 * heads)).reshape(1, GATE_LANES)
    gates = _gates(x, w_gate, b_gate)

    zeros_c = jnp.zeros((bp, heads, dk, dv), F32)
    hn_p, c_p, n_p, m_p = _mlstm(
        proj[None], gates[None],
        zeros_c, jnp.zeros((bp, heads, dk), F32), jnp.zeros((bp, heads), F32),
        hn_gain_a[0], bsz=bp, seq_len=tp, heads=heads, dk=dk, dv=dv, chunk=MLSTM_CHUNK,
        name="mlstm_prompt")

    pad_t = SAMPLE_PAD_LEN - ts
    proj_s = jnp.pad(proj[n_prompt:].reshape(bs, ts, n_main), ((0, 0), (0, pad_t), (0, 0)))
    lane = jnp.arange(GATE_LANES)
    pad_gate = jnp.where(lane < heads, NEG_BIG, jnp.where(lane < 2 * heads, POS_BIG, 0.0))
    gates_s = jnp.concatenate(
        [gates[n_prompt:].reshape(bs, ts, GATE_LANES),
         jnp.broadcast_to(pad_gate.astype(F32), (bs, pad_t, GATE_LANES))], axis=1)
    hn_s, c_s, n_s, m_s = _mlstm(
        proj_s, gates_s, state_mlstm_C[0], state_mlstm_n[0], state_mlstm_m[0],
        hn_gain_a[0], bsz=bs, seq_len=SAMPLE_PAD_LEN, heads=heads, dk=dk, dv=dv,
        chunk=SAMPLE_PAD_LEN, name="mlstm_sample")

    hn = jnp.concatenate([hn_p[0], hn_s[:, :ts].reshape(n_sample, hv)])
    x, xt = _matmul_res_ln(hn, w_out_a[0].astype(BF16), x, ln_mix_g[0], ln_mix_b[0],
                           alpha=alpha, name="mlstm_out_proj_ln")
    exp_u_all = peer_u.astype(BF16)
    n_layers, n_exp, _ = peer_v.shape
    exp_block = PEER_EXPERT_ROWS * peer_sub_keys.shape[3]
    exp_vt_all = jnp.swapaxes(peer_v.reshape(n_layers, n_exp // exp_block, exp_block, d),
                              2, 3).astype(BF16)
    x, xb = _peer(x, xt, peer_w_q[0], peer_sub_keys[0], exp_u_all, exp_vt_all,
                  ln_ffn_g[0], ln_ffn_b[0], alpha=alpha, layer=0)

    half = w_in_b.shape[-1] // 2
    groups, chunk = w_s_b.shape[1], w_s_b.shape[2]
    assert tp % chunk == 0 and n_sample % chunk == 0 and chunk % ts == 0
    w_in = w_in_b[0].astype(BF16)
    u = _matmul(xb, w_in[:, :half], b_in_b[0][:half], act="gelu", out_dtype=BF16,
                name="gmlp_in_proj_u")
    v = _matmul(xb, w_in[:, half:], b_in_b[0][half:], act="gelu", name="gmlp_in_proj_v")
    ws = jnp.where(jnp.tril(jnp.ones((chunk, chunk), dtype=bool)), w_s_b[0], 0.0)
    ws_sample = jnp.einsum("ab,gts->gatbs", jnp.eye(chunk // ts, dtype=F32),
                           ws[:, :ts, :ts]).reshape(groups, chunk, chunk)
    wmix = jnp.stack([ws, ws_sample])
    bmix = jnp.stack([b_s_b[0].T, jnp.tile(b_s_b[0][:, :ts].T, (chunk // ts, 1))])
    prod, v_s = _gmlp_mix(u, v, wmix, bmix, lnv_g_b[0], lnv_b_b[0],
                          n_prompt_chunks=n_prompt // chunk, n_sample_rows=n_sample)
    x, xt = _matmul_res_ln(prod, w_out_b[0].astype(BF16), x, ln_mix_g[1], ln_mix_b[1],
                           alpha=alpha, name="gmlp_out_proj_ln")
    x, xb = _peer(x, xt, peer_w_q[1], peer_sub_keys[1], exp_u_all, exp_vt_all,
                  ln_ffn_g[1], ln_ffn_b[1], alpha=alpha, layer=1)

    return (x[:n_prompt].reshape(bp, tp, d), x[n_prompt:].reshape(bs, ts, d),
            c_p[None], n_p[None], m_p[None], c_s[None], n_s[None], m_s[None],
            v_s.reshape(1, bs, ts, half))
```

```python
import functools
import math

import jax
import jax.numpy as jnp
from jax import lax
from jax.experimental import pallas as pl
from jax.experimental.pallas import tpu as pltpu

F32 = jnp.float32
BF16 = jnp.bfloat16

LN_EPS = 1e-5
PEER_TOPK = 16
MLSTM_CHUNK = 256
SAMPLE_PAD_LEN = 8
MLSTM_STAGE_MAJOR_MAX_CHUNK = 256
GATE_LANES = 128
LANES = 128
F32_SUBLANES = 8
BF16_SUBLANES = 16
NOT_RANKED = 255.0
V7X_VMEM_LIMIT_BYTES = 48 * 1024 * 1024
PEER_EXPERT_VMEM_LIMIT_BYTES = 56 * 1024 * 1024
PEER_EXPERT_ROWS = 8
SELECT_Q_SLAB_ROWS = 512
MM_SLAB = 512
MM_DEPTH = 256
NEG_BIG = -1e30
POS_BIG = 1e30


def _params(*sem):
    return pltpu.CompilerParams(dimension_semantics=sem,
                                vmem_limit_bytes=V7X_VMEM_LIMIT_BYTES)


def _pick_tile(n, candidates):
    for c in candidates:
        if n % c == 0:
            return c
    raise ValueError(f"no tile in {candidates} divides {n}")


def _layer_norm(z, g, b):
    mu = jnp.mean(z, axis=-1, keepdims=True)
    zc = z - mu
    var = jnp.mean(zc * zc, axis=-1, keepdims=True)
    return zc * lax.rsqrt(var + LN_EPS) * g + b


def _gelu_tanh(x):
    k0 = -2.0 * math.sqrt(2.0 / math.pi) * math.log2(math.e)
    k1 = 0.044715 * k0
    return x / (1.0 + jnp.exp2(x * (x * x * k1 + k0)))


def _log_sigmoid(x):
    return jnp.minimum(x, 0.0) - jnp.log1p(jnp.exp(-jnp.abs(x)))


def _mm_kernel(x_ref, w_ref, b_ref, o_ref, *, act):
    acc = jnp.dot(x_ref[...], w_ref[...], preferred_element_type=F32)
    acc = acc + b_ref[...]
    if act == "gelu":
        acc = _gelu_tanh(acc)
    o_ref[...] = acc.astype(o_ref.dtype)


def _matmul(x, w, bias, *, act=None, out_dtype=F32, name):
    m, k = x.shape
    n = w.shape[1]
    tm = _pick_tile(m, (512, 256, 128))
    tn = _pick_tile(n, (1024, 512, 256, 128))
    return pl.pallas_call(
        functools.partial(_mm_kernel, act=act),
        out_shape=jax.ShapeDtypeStruct((m, n), out_dtype),
        grid=(n // tn, m // tm),
        in_specs=[
            pl.BlockSpec((tm, k), lambda j, i: (i, 0)),
            pl.BlockSpec((k, tn), lambda j, i: (0, j)),
            pl.BlockSpec((1, tn), lambda j, i: (0, j)),
        ],
        out_specs=pl.BlockSpec((tm, tn), lambda j, i: (i, j)),
        compiler_params=_params("parallel", "parallel"),
        name=name,
    )(x, w, bias.reshape(1, n).astype(F32))


def _gate_kernel(x_ref, w_ref, b_ref, o_ref):
    x = x_ref[...]
    w = w_ref[...]
    xh = x.astype(BF16)
    xl = (x - xh.astype(F32)).astype(BF16)
    wh = w.astype(BF16)
    wl = (w - wh.astype(F32)).astype(BF16)
    acc = jnp.dot(xh, wh, preferred_element_type=F32)
    acc = acc + jnp.dot(xl, wh, preferred_element_type=F32)
    acc = acc + jnp.dot(xh, wl, preferred_element_type=F32)
    o_ref[...] = acc + b_ref[...]


def _gates(x, w_gate, b_gate):
    m, k = x.shape
    tm = _pick_tile(m, (512, 256, 128))
    return pl.pallas_call(
        _gate_kernel,
        out_shape=jax.ShapeDtypeStruct((m, GATE_LANES), F32),
        grid=(m // tm,),
        in_specs=[
            pl.BlockSpec((tm, k), lambda i: (i, 0)),
            pl.BlockSpec((k, GATE_LANES), lambda i: (0, 0)),
            pl.BlockSpec((1, GATE_LANES), lambda i: (0, 0)),
        ],
        out_specs=pl.BlockSpec((tm, GATE_LANES), lambda i: (i, 0)),
        compiler_params=_params("parallel"),
        name="mlstm_gates",
    )(x, w_gate, b_gate)


def _mlstm_kernel(q_ref, k_ref, v_ref, o_ref, g_ref, gt_ref, c0_ref, n0_ref,
                  m0_ref, gain_ref, hn_ref, c_ref, n_ref, m_ref,
                  *, heads, dk, dv, chunk, stage_major):
    @pl.when(pl.program_id(1) == 0)
    def _():
        c_ref[...] = c0_ref[...]
        n_ref[...] = n0_ref[...]
        m_ref[...] = m0_ref[...]

    g = g_ref[0]
    gt = gt_ref[0]
    row = lax.broadcasted_iota(jnp.int32, (chunk, chunk), 0)
    col = lax.broadcasted_iota(jnp.int32, (chunk, chunk), 1)
    causal = row >= col
    tril = causal.astype(F32)
    triu = (row <= col).astype(F32)
    lf_col = _log_sigmoid(g)
    lf_row = _log_sigmoid(gt)
    b_col_all = jnp.dot(tril, lf_col, precision=lax.Precision.HIGHEST,
                        preferred_element_type=F32)
    b_row_all = jnp.dot(lf_row, triu, precision=lax.Precision.HIGHEST,
                        preferred_element_type=F32)
    k_scale = dk ** -0.5

    def gate_terms(h):
        bc = b_col_all[:, heads + h:heads + h + 1]
        br = b_row_all[heads + h:heads + h + 1, :]
        ic = g[:, h:h + 1]
        ir = gt[h:h + 1, :]
        m_prev = m_ref[0, h][:, 0:1]
        dlog = jnp.where(causal, bc - br + ir, -jnp.inf)
        inter = bc + m_prev
        m_t = jnp.maximum(inter, jnp.max(dlog, axis=-1, keepdims=True))
        b_end = bc[chunk - 1:chunk, :]
        m_new = jnp.maximum(b_end + m_prev,
                            jnp.max(b_end - br + ir, axis=-1, keepdims=True))
        return dict(decay_mat=jnp.exp(dlog - m_t), w_inter=jnp.exp(inter - m_t),
                    floor=jnp.exp(-m_t), m_new=m_new,
                    w_k=jnp.exp(b_end - bc + ic - m_new),
                    decay=jnp.exp(b_end + m_prev - m_new))

    def load_qkv(h):
        qh = q_ref[0, :, h * dk:(h + 1) * dk]
        kh = k_ref[0, :, h * dk:(h + 1) * dk] * k_scale
        vb = v_ref[0, :, h * dv:(h + 1) * dv].astype(BF16)
        return qh, kh, vb

    def scores(qh, kh):
        return lax.dot_general(qh.astype(BF16), kh.astype(BF16), (((1,), (1,)), ((), ())),
                               preferred_element_type=F32)

    def read_out(h, qh, vb, qk, t):
        c_prev = c_ref[0, h]
        n_prev = n_ref[0, h]
        s = qk * t["decay_mat"]
        num = jnp.dot(s.astype(BF16), vb, preferred_element_type=F32)
        num = num + t["w_inter"] * jnp.dot(qh.astype(BF16), c_prev.astype(BF16),
                                           preferred_element_type=F32)
        den = jnp.sum(s, axis=-1, keepdims=True)
        den = den + t["w_inter"] * jnp.sum(qh * n_prev, axis=-1, keepdims=True)
        return num / jnp.maximum(jnp.abs(den), t["floor"])

    def gate_and_norm(h, hval):
        hg = hval * jax.nn.sigmoid(o_ref[0, :, h * dv:(h + 1) * dv])
        mu = jnp.mean(hg, axis=-1, keepdims=True)
        hc = hg - mu
        var = jnp.mean(hc * hc, axis=-1, keepdims=True)
        hn = hc * lax.rsqrt(var + LN_EPS) * gain_ref[h]
        hn_ref[0, :, h * dv:(h + 1) * dv] = hn.astype(hn_ref.dtype)

    def update_state(h, kh, vb, t):
        kw = kh * t["w_k"]
        c_ref[0, h] = t["decay"] * c_ref[0, h] + lax.dot_general(
            kw.astype(BF16), vb, (((0,), (0,)), ((), ())), preferred_element_type=F32)
        n_ref[0, h] = t["decay"] * n_ref[0, h] + jnp.sum(kw, axis=0, keepdims=True)
        m_ref[0, h] = jnp.broadcast_to(t["m_new"], (1, GATE_LANES))

    if stage_major:
        qkv = [load_qkv(h) for h in range(heads)]
        qks = [scores(qh, kh) for qh, kh, _ in qkv]
        terms = [gate_terms(h) for h in range(heads)]
        hvals = [read_out(h, qkv[h][0], qkv[h][2], qks[h], terms[h]) for h in range(heads)]
        for h in range(heads):
            update_state(h, qkv[h][1], qkv[h][2], terms[h])
        for h in range(heads):
            gate_and_norm(h, hvals[h])
    else:
        for h in range(heads):
            qh, kh, vb = load_qkv(h)
            t = gate_terms(h)
            gate_and_norm(h, read_out(h, qh, vb, scores(qh, kh), t))
            update_state(h, kh, vb, t)


def _mlstm(proj, gates, c0, n0, m0, gain, *, bsz, seq_len, heads, dk, dv, chunk, name):
    hk, hv = heads * dk, heads * dv
    chunks = seq_len // chunk
    if proj.shape[0] == 1:
        rows = lambda b, c: (0, b * chunks + c)
    else:
        rows = lambda b, c: (b, c)
    gt = jnp.swapaxes(gates[..., :2 * heads], 1, 2)
    n0 = n0.reshape(bsz, heads, 1, dk)
    m0 = jnp.broadcast_to(m0[..., None, None], (bsz, heads, 1, GATE_LANES))
    state_spec = lambda shape: pl.BlockSpec((1,) + shape, lambda b, c: (b, 0, 0, 0))
    tok_spec = lambda width, col: pl.BlockSpec((1, chunk, width), lambda b, c: rows(b, c) + (col,))
    hn, c, n, m = pl.pallas_call(
        functools.partial(_mlstm_kernel, heads=heads, dk=dk, dv=dv, chunk=chunk,
                          stage_major=chunk <= MLSTM_STAGE_MAJOR_MAX_CHUNK),
        out_shape=(
            jax.ShapeDtypeStruct((proj.shape[0], bsz * seq_len // proj.shape[0], hv), BF16),
            jax.ShapeDtypeStruct((bsz, heads, dk, dv), F32),
            jax.ShapeDtypeStruct((bsz, heads, 1, dk), F32),
            jax.ShapeDtypeStruct((bsz, heads, 1, GATE_LANES), F32),
        ),
        grid=(bsz, chunks),
        in_specs=[
            tok_spec(hk, 0),
            tok_spec(hk, 1),
            tok_spec(hv, 1),
            tok_spec(hv, 2),
            tok_spec(GATE_LANES, 0),
            pl.BlockSpec((1, 2 * heads, chunk),
                         lambda b, c: (rows(b, c)[0], 0, rows(b, c)[1])),
            state_spec((heads, dk, dv)),
            state_spec((heads, 1, dk)),
            state_spec((heads, 1, GATE_LANES)),
            pl.BlockSpec((heads, 1, dv), lambda b, c: (0, 0, 0)),
        ],
        out_specs=(
            tok_spec(hv, 0),
            state_spec((heads, dk, dv)),
            state_spec((heads, 1, dk)),
            state_spec((heads, 1, GATE_LANES)),
        ),
        compiler_params=_params("parallel", "arbitrary"),
        name=name,
    )(proj, proj, proj, proj, gates, gt, c0, n0, m0, gain.reshape(heads, 1, dv))
    return hn, c, n.reshape(bsz, heads, dk), m[:, :, 0, 0]


def _mm_res_ln_kernel(a_ref, w_ref, x_ref, g_ref, b_ref, o_ref, obt_ref, *, alpha):
    y = jnp.dot(a_ref[...], w_ref[...], preferred_element_type=F32)
    out = _layer_norm(alpha * x_ref[...] + y, g_ref[...], b_ref[...])
    o_ref[...] = out
    obt_ref[...] = out.T.astype(BF16)


def _matmul_res_ln(a, w, x, g, b, *, alpha, name):
    m, k = a.shape
    d = w.shape[1]
    tm = _pick_tile(m, (256, 128))
    row = lambda i: (i, 0)
    const = lambda i: (0, 0)
    return pl.pallas_call(
        functools.partial(_mm_res_ln_kernel, alpha=alpha),
        out_shape=(jax.ShapeDtypeStruct((m, d), F32),
                   jax.ShapeDtypeStruct((d, m), BF16)),
        grid=(m // tm,),
        in_specs=[
            pl.BlockSpec((tm, k), row),
            pl.BlockSpec((k, d), const, pipeline_mode=pl.Buffered(1)),
            pl.BlockSpec((tm, d), row),
            pl.BlockSpec((1, d), const),
            pl.BlockSpec((1, d), const),
        ],
        out_specs=(pl.BlockSpec((tm, d), row), pl.BlockSpec((d, tm), lambda i: (0, i))),
        compiler_params=_params("parallel"),
        name=name,
    )(a, w, x, g.reshape(1, d), b.reshape(1, d))


def _res_ln_kernel(x_ref, yt_ref, g_ref, b_ref, o_ref, ob_ref, *, alpha):
    out = _layer_norm(alpha * x_ref[...] + yt_ref[...].T, g_ref[...], b_ref[...])
    o_ref[...] = out
    ob_ref[...] = out.astype(BF16)


def _res_ln(x, yt, g, b, *, alpha, name):
    m, d = x.shape
    tm = _pick_tile(m, (256, 128))
    row = lambda i: (i, 0)
    const = lambda i: (0, 0)
    return pl.pallas_call(
        functools.partial(_res_ln_kernel, alpha=alpha),
        out_shape=(jax.ShapeDtypeStruct((m, d), F32),
                   jax.ShapeDtypeStruct((m, d), BF16)),
        grid=(m // tm,),
        in_specs=[pl.BlockSpec((tm, d), row), pl.BlockSpec((d, tm), lambda i: (0, i)),
                  pl.BlockSpec((1, d), const), pl.BlockSpec((1, d), const)],
        out_specs=(pl.BlockSpec((tm, d), row), pl.BlockSpec((tm, d), row)),
        compiler_params=_params("parallel"),
        name=name,
    )(x, yt, g.reshape(1, d), b.reshape(1, d))


def _gmlp_mix_kernel(u_ref, v_ref, wmix_ref, bmix_ref, lg_ref, lb_ref,
                     prod_ref, vln_ref, *, groups, gdim):
    vln = _layer_norm(v_ref[...], lg_ref[...], lb_ref[...])
    vln_ref[...] = vln
    for g in range(groups):
        sl = slice(g * gdim, (g + 1) * gdim)
        mixed = jnp.dot(wmix_ref[0, g].astype(BF16), vln[:, sl].astype(BF16),
                        preferred_element_type=F32)
        mixed = mixed + bmix_ref[0, :, g:g + 1]
        prod_ref[:, sl] = (u_ref[:, sl].astype(F32) * mixed).astype(prod_ref.dtype)


def _gmlp_mix(u, v, wmix, bmix, lnv_g, lnv_b, *, n_prompt_chunks, n_sample_rows):
    n, half = v.shape
    groups, chunk = wmix.shape[1], wmix.shape[2]
    gdim = half // groups
    kind = lambda i: jnp.where(i >= n_prompt_chunks, 1, 0)
    row = lambda i: (i, 0)
    const = lambda i: (0, 0)
    return pl.pallas_call(
        functools.partial(_gmlp_mix_kernel, groups=groups, gdim=gdim),
        out_shape=(jax.ShapeDtypeStruct((n, half), BF16),
                   jax.ShapeDtypeStruct((n_sample_rows, half), F32)),
        grid=(n // chunk,),
        in_specs=[
            pl.BlockSpec((chunk, half), row),
            pl.BlockSpec((chunk, half), row),
            pl.BlockSpec((1, groups, chunk, chunk), lambda i: (kind(i), 0, 0, 0)),
            pl.BlockSpec((1, chunk, groups), lambda i: (kind(i), 0, 0)),
            pl.BlockSpec((1, half), const),
            pl.BlockSpec((1, half), const),
        ],
        out_specs=(
            pl.BlockSpec((chunk, half), row),
            pl.BlockSpec((chunk, half),
                         lambda i: (jnp.maximum(i - n_prompt_chunks, 0), 0)),
        ),
        compiler_params=_params("arbitrary"),
        name="gmlp_mix",
    )(u, v, wmix, bmix, lnv_g.reshape(1, half), lnv_b.reshape(1, half))


def _peer_select_kernel(xt_ref, wqt_ref, sk_ref, cnt0_ref, pz0_ref, rank1_ref, p1_ref,
                        qt_ref, s0_ref, tops_ref, *, heads, nkeys):
    tm = qt_ref.shape[1]
    slab_heads = max(1, SELECT_Q_SLAB_ROWS // (2 * nkeys))
    slab_rows = slab_heads * 2 * nkeys

    def project(first_head):
        r = slice(first_head * 2 * nkeys, first_head * 2 * nkeys + slab_rows)
        qt_ref[r, :] = jnp.dot(wqt_ref[r, :], xt_ref[...], preferred_element_type=F32)

    project(0)
    for h in range(heads):
        if h % slab_heads == 0 and h + slab_heads < heads:
            project(h + slab_heads)
        for p in range(2):
            r0 = (2 * h + p) * nkeys
            qhp = qt_ref[r0:r0 + nkeys, :].astype(BF16)
            s = jnp.dot(sk_ref[h, p], qhp, preferred_element_type=F32)
            cur = jnp.full((1, tm), jnp.inf, F32)
            rank = jnp.full(s.shape, NOT_RANKED, F32)
            for r in range(PEER_TOPK):
                cur = jnp.max(jnp.where(s < cur, s, -jnp.inf), axis=0, keepdims=True)
                tops_ref[p, r, h:h + 1, :] = cur
                if p == 1:
                    rank = jnp.where(s == cur, float(r), rank)
            if p == 0:
                s0_ref[h] = s
            else:
                rank1_ref[h] = pltpu.bitcast(rank.astype(BF16), jnp.uint32)
                p1_ref[h] = pltpu.bitcast(
                    jnp.exp(s - tops_ref[1, 0, h:h + 1, :]).astype(BF16), jnp.uint32)

    a = [tops_ref[0, r] for r in range(PEER_TOPK)]
    b = [tops_ref[1, r] for r in range(PEER_TOPK)]
    cands = [a[k] + b[l] for k in range(PEER_TOPK) for l in range(PEER_TOPK)
             if (k + 1) * (l + 1) <= PEER_TOPK]
    cur = jnp.full(cands[0].shape, jnp.inf, F32)
    for _ in range(PEER_TOPK):
        nxt = jnp.full(cands[0].shape, -jnp.inf, F32)
        for c in cands:
            nxt = jnp.maximum(nxt, jnp.where(c < cur, c, -jnp.inf))
        cur = nxt
    thr = cur
    top = cands[0]
    z = jnp.zeros(thr.shape, F32)
    for c in cands:
        z = z + jnp.where(c >= thr, jnp.exp(c - top), 0.0)
    inv_z = 1.0 / z
    for h in range(heads):
        s0 = s0_ref[h]
        thr_h = thr[h:h + 1, :]
        b_h = [b[l][h:h + 1, :] for l in range(PEER_TOPK)]
        cnt = jnp.zeros(s0.shape, F32)
        step = PEER_TOPK // 2
        while step >= 1:
            probe = b_h[step - 1]
            for base in range(2 * step, PEER_TOPK, 2 * step):
                probe = jnp.where(cnt == float(base), b_h[base + step - 1], probe)
            cnt = cnt + jnp.where(s0 + probe >= thr_h, float(step), 0.0)
            step //= 2
        cnt = jnp.where(s0 + b_h[PEER_TOPK - 1] >= thr_h, float(PEER_TOPK), cnt)
        cnt0_ref[h] = cnt
        pz0_ref[h] = jnp.exp(s0 - a[0][h:h + 1, :]) * inv_z[h:h + 1, :]


def _peer_select(xt, wqt, sk):
    d, n = xt.shape
    heads, _, nkeys, _ = sk.shape
    tm = _pick_tile(n, (256, 128))
    key_spec = pl.BlockSpec((heads, nkeys, tm), lambda t: (0, 0, t))
    pair_spec = pl.BlockSpec((heads, nkeys // 2, tm), lambda t: (0, 0, t))
    return pl.pallas_call(
        functools.partial(_peer_select_kernel, heads=heads, nkeys=nkeys),
        out_shape=(jax.ShapeDtypeStruct((heads, nkeys, n), F32),
                   jax.ShapeDtypeStruct((heads, nkeys, n), F32),
                   jax.ShapeDtypeStruct((heads, nkeys // 2, n), jnp.uint32),
                   jax.ShapeDtypeStruct((heads, nkeys // 2, n), jnp.uint32)),
        grid=(n // tm,),
        in_specs=[
            pl.BlockSpec((d, tm), lambda t: (0, t)),
            pl.BlockSpec(wqt.shape, lambda t: (0, 0), pipeline_mode=pl.Buffered(1)),
            pl.BlockSpec(sk.shape, lambda t: (0, 0, 0, 0)),
        ],
        out_specs=(key_spec, key_spec, pair_spec, pair_spec),
        scratch_shapes=[pltpu.VMEM((wqt.shape[0], tm), F32),
                        pltpu.VMEM((heads, nkeys, tm), F32),
                        pltpu.VMEM((2, PEER_TOPK, heads, tm), F32)],
        compiler_params=_params("parallel"),
        name="peer_select",
    )(xt, wqt, sk)


def _peer_expert_kernel(xt_ref, u_ref, vt_ref, cnt0_ref, pz0_ref, rank1_ref, p1_ref,
                        out_ref, at0_ref, at1_ref, wt0_ref, wt1_ref, cnt_s, pz_s,
                        *, heads, nkeys, n_blocks, n_items):
    s = pl.program_id(0)
    eb, tm = at0_ref.shape
    rows = eb // nkeys

    @pl.when(s == 0)
    def _():
        at1_ref[...] = jnp.zeros_like(at1_ref)
        wt0_ref[...] = jnp.zeros_like(wt0_ref)

    @pl.when((s <= 2) | (lax.rem(s - 2, n_blocks) == 0))
    def _():
        out_ref[...] = jnp.zeros_like(out_ref)

    item_b = jnp.clip(s - 1, 0, n_items - 1)
    valid_b = jnp.where((s >= 1) & (s <= n_items), 1.0, 0.0).astype(F32)
    first_row = pl.multiple_of(lax.rem(item_b, n_blocks) * rows, rows)

    d_model = out_ref.shape[0]
    reps = nkeys // BF16_SUBLANES
    lane_tiles = tm // LANES

    def routing_piece(at_r, wt_w, ii, tc):
        keys = slice(ii * nkeys, (ii + 1) * nkeys)
        lanes = slice(tc * LANES, (tc + 1) * LANES)
        g = jnp.zeros((nkeys, LANES), BF16)
        for h in range(heads):
            cnt = pltpu.bitcast(pltpu.repeat(cnt_s[h, ii, :, lanes], reps, axis=0), BF16)
            pz = pltpu.bitcast(pltpu.repeat(pz_s[h, ii, :, lanes], reps, axis=0), BF16)
            r1 = pltpu.bitcast(rank1_ref[h, :, lanes], BF16)
            p1 = pltpu.bitcast(p1_ref[h, :, lanes], BF16)
            g = g + jnp.where(r1 < cnt, p1 * pz, jnp.zeros_like(p1))
        wt_w[keys, lanes] = (_gelu_tanh(at_r[keys, lanes]) * g.astype(F32)).astype(BF16)

    def stage_a_piece(at_w, slab, kc):
        r = slice(slab * MM_SLAB, (slab + 1) * MM_SLAB)
        k = slice(kc * MM_DEPTH, (kc + 1) * MM_DEPTH)
        part = jnp.dot(u_ref[r, k], xt_ref[k, :], preferred_element_type=F32)
        if kc == 0:
            at_w[r, :] = part
        else:
            at_w[r, :] += part

    def stage_c_piece(wt_r, slab, kc):
        r = slice(slab * MM_SLAB, (slab + 1) * MM_SLAB)
        k = slice(kc * MM_DEPTH, (kc + 1) * MM_DEPTH)
        out_ref[r, :] += jnp.dot(vt_ref[r, k], wt_r[k, :], preferred_element_type=F32)

    def step_interleaved(at_w, at_r, wt_w, wt_r):
        stage_rows(first_row)
        b_pieces = [(ii, tc) for ii in range(rows) for tc in range(lane_tiles)]
        a_pieces = [(slab, kc) for slab in range(eb // MM_SLAB)
                    for kc in range(d_model // MM_DEPTH)]
        c_pieces = [(slab, kc) for slab in range(d_model // MM_SLAB)
                    for kc in range(eb // MM_DEPTH)]
        mxu_pieces = []
        for i in range(max(len(a_pieces), len(c_pieces))):
            if i < len(c_pieces):
                mxu_pieces.append(("c", c_pieces[i]))
            if i < len(a_pieces):
                mxu_pieces.append(("a", a_pieces[i]))
        n_slots = max(len(b_pieces), len(mxu_pieces))
        for slot in range(n_slots):
            for ii, tc in b_pieces[slot * len(b_pieces) // n_slots:
                                   (slot + 1) * len(b_pieces) // n_slots]:
                routing_piece(at_r, wt_w, ii, tc)
            for kind, (slab, kc) in mxu_pieces[slot * len(mxu_pieces) // n_slots:
                                               (slot + 1) * len(mxu_pieces) // n_slots]:
                if kind == "a":
                    stage_a_piece(at_w, slab, kc)
                else:
                    stage_c_piece(wt_r, slab, kc)

    def stage_rows(first_row):
        for h in range(heads):
            cnt_rows = cnt0_ref[h, pl.ds(first_row, rows), :]
            pz_rows = pz0_ref[h, pl.ds(first_row, rows), :] * valid_b
            for ii in range(rows):
                cnt_s[h, ii] = pltpu.bitcast(
                    jnp.broadcast_to(cnt_rows[ii:ii + 1, :], (BF16_SUBLANES, tm)).astype(BF16),
                    jnp.uint32)
                pz_s[h, ii] = pltpu.bitcast(
                    jnp.broadcast_to(pz_rows[ii:ii + 1, :], (BF16_SUBLANES, tm)).astype(BF16),
                    jnp.uint32)

    @pl.when(lax.rem(s, 2) == 0)
    def _():
        step_interleaved(at0_ref, at1_ref, wt1_ref, wt0_ref)

    @pl.when(lax.rem(s, 2) == 1)
    def _():
        step_interleaved(at1_ref, at0_ref, wt0_ref, wt1_ref)


def _peer_expert(xt, u, vt, cnt0, pz0, rank1, p1, *, layer):
    d, n = xt.shape
    n_exp = u.shape[1]
    heads, nkeys, _ = cnt0.shape
    tm = _pick_tile(n, (512, 256, 128))
    eb = PEER_EXPERT_ROWS * nkeys
    n_blocks = n_exp // eb
    n_items = (n // tm) * n_blocks

    def item(s, lag):
        return jnp.clip(s - lag, 0, n_items - 1)

    tile_b = lambda s: item(s, 1) // n_blocks
    row_spec = pl.BlockSpec((heads, nkeys, tm), lambda s: (0, 0, tile_b(s)))
    pair_spec = pl.BlockSpec((heads, nkeys // 2, tm), lambda s: (0, 0, tile_b(s)))
    return pl.pallas_call(
        functools.partial(_peer_expert_kernel, heads=heads, nkeys=nkeys,
                          n_blocks=n_blocks, n_items=n_items),
        out_shape=jax.ShapeDtypeStruct((d, n), F32),
        grid=(n_items + 2,),
        in_specs=[
            pl.BlockSpec((d, tm), lambda s: (0, item(s, 0) // n_blocks)),
            pl.BlockSpec((None, eb, d), lambda s: (layer, item(s, 0) % n_blocks, 0)),
            pl.BlockSpec((None, None, d, eb), lambda s: (layer, item(s, 2) % n_blocks, 0, 0)),
            row_spec, row_spec, pair_spec, pair_spec,
        ],
        out_specs=pl.BlockSpec((d, tm), lambda s: (0, item(s, 2) // n_blocks)),
        scratch_shapes=[pltpu.VMEM((eb, tm), F32), pltpu.VMEM((eb, tm), F32),
                        pltpu.VMEM((eb, tm), BF16), pltpu.VMEM((eb, tm), BF16),
                        pltpu.VMEM((heads, PEER_EXPERT_ROWS, F32_SUBLANES, tm), jnp.uint32),
                        pltpu.VMEM((heads, PEER_EXPERT_ROWS, F32_SUBLANES, tm), jnp.uint32)],
        compiler_params=pltpu.CompilerParams(
            dimension_semantics=("arbitrary",), vmem_limit_bytes=PEER_EXPERT_VMEM_LIMIT_BYTES),
        name="peer_expert",
    )(xt, u, vt, cnt0, pz0, rank1, p1)


def _peer(x, xt, w_q, sub_keys, exp_u_all, exp_vt_all, ln_g, ln_b, *, alpha, layer):
    wqt = w_q.T.astype(BF16)
    sk = sub_keys.astype(BF16)
    cnt0, pz0, rank1, p1 = _peer_select(xt, wqt, sk)
    out_t = _peer_expert(xt, exp_u_all, exp_vt_all, cnt0, pz0, rank1, p1, layer=layer)
    return _res_ln(x, out_t, ln_g, ln_b, alpha=alpha, name=f"peer_ln_{layer}")


def kernel(x_prompt, x_sample, state_mlstm_C, state_mlstm_n, state_mlstm_m,
           w_in_a, b_gate_a, hn_gain_a, w_out_a, w_in_b, b_in_b, lnv_g_b, lnv_b_b,
           w_s_b, b_s_b, w_out_b, ln_mix_g, ln_mix_b, ln_ffn_g, ln_ffn_b,
           peer_w_q, peer_sub_keys, peer_u, peer_v):
    bp, tp, d = x_prompt.shape
    bs, ts, _ = x_sample.shape
    depth = ln_mix_g.shape[0]
    alpha = float((2 * depth) ** 0.25)
    heads = b_gate_a.shape[-1] // 2
    dv = hn_gain_a.shape[-1]
    dk = state_mlstm_n.shape[-1]
    hk, hv = heads * dk, heads * dv
    n_prompt, n_sample = bp * tp, bs * ts
    assert depth == 2 and w_in_a.shape[0] == 1 and w_in_b.shape[0] == 1
    assert ts <= SAMPLE_PAD_LEN and tp % MLSTM_CHUNK == 0

    x = jnp.concatenate([x_prompt.reshape(n_prompt, d), x_sample.reshape(n_sample, d)])
    xb = x.astype(BF16)

    n_main = 2 * hk + 2 * hv
    proj = _matmul(xb, w_in_a[0][:, :n_main].astype(BF16), jnp.zeros((n_main,), F32),
                   name="mlstm_in_proj")
    w_gate = jnp.pad(w_in_a[0][:, n_main:], ((0, 0), (0, GATE_LANES - 2 * heads)))
    b_gate = jnp.pad(b_gate_a[0], (0, GATE_LANES - 2 * heads)).reshape(1, GATE_LANES)
    gates = _gates(x, w_gate, b_gate)

    zeros_c = jnp.zeros((bp, heads, dk, dv), F32)
    hn_p, c_p, n_p, m_p = _mlstm(
        proj[None], gates[None],
        zeros_c, jnp.zeros((bp, heads, dk), F32), jnp.zeros((bp, heads), F32),
        hn_gain_a[0], bsz=bp, seq_len=tp, heads=heads, dk=dk, dv=dv, chunk=MLSTM_CHUNK,
        name="mlstm_prompt")

    pad_t = SAMPLE_PAD_LEN - ts
    proj_s = jnp.pad(proj[n_prompt:].reshape(bs, ts, n_main), ((0, 0), (0, pad_t), (0, 0)))
    lane = jnp.arange(GATE_LANES)
    pad_gate = jnp.where(lane < heads, NEG_BIG, jnp.where(lane < 2 * heads, POS_BIG, 0.0))
    gates_s = jnp.concatenate(
        [gates[n_prompt:].reshape(bs, ts, GATE_LANES),
         jnp.broadcast_to(pad_gate.astype(F32), (bs, pad_t, GATE_LANES))], axis=1)
    hn_s, c_s, n_s, m_s = _mlstm(
        proj_s, gates_s, state_mlstm_C[0], state_mlstm_n[0], state_mlstm_m[0],
        hn_gain_a[0], bsz=bs, seq_len=SAMPLE_PAD_LEN, heads=heads, dk=dk, dv=dv,
        chunk=SAMPLE_PAD_LEN, name="mlstm_sample")

    hn = jnp.concatenate([hn_p[0], hn_s[:, :ts].reshape(n_sample, hv)])
    x, xt = _matmul_res_ln(hn, w_out_a[0].astype(BF16), x, ln_mix_g[0], ln_mix_b[0],
                           alpha=alpha, name="mlstm_out_proj_ln")
    exp_u_all = peer_u.astype(BF16)
    n_layers, n_exp, _ = peer_v.shape
    exp_block = PEER_EXPERT_ROWS * peer_sub_keys.shape[3]
    exp_vt_all = jnp.swapaxes(peer_v.reshape(n_layers, n_exp // exp_block, exp_block, d),
                              2, 3).astype(BF16)
    x, xb = _peer(x, xt, peer_w_q[0], peer_sub_keys[0], exp_u_all, exp_vt_all,
                  ln_ffn_g[0], ln_ffn_b[0], alpha=alpha, layer=0)

    half = w_in_b.shape[-1] // 2
    groups, chunk = w_s_b.shape[1], w_s_b.shape[2]
    assert tp % chunk == 0 and n_sample % chunk == 0 and chunk % ts == 0
    w_in = w_in_b[0].astype(BF16)
    u = _matmul(xb, w_in[:, :half], b_in_b[0][:half], act="gelu", out_dtype=BF16,
                name="gmlp_in_proj_u")
    v = _matmul(xb, w_in[:, half:], b_in_b[0][half:], act="gelu", name="gmlp_in_proj_v")
    ws = jnp.where(jnp.tril(jnp.ones((chunk, chunk), dtype=bool)), w_s_b[0], 0.0)
    ws_sample = jnp.einsum("ab,gts->gatbs", jnp.eye(chunk // ts, dtype=F32),
                           ws[:, :ts, :ts]).reshape(groups, chunk, chunk)
    wmix = jnp.stack([ws, ws_sample])
    bmix = jnp.stack([b_s_b[0].T, jnp.tile(b_s_b[0][:, :ts].T, (chunk // ts, 1))])
    prod, v_s = _gmlp_mix(u, v, wmix, bmix, lnv_g_b[0], lnv_b_b[0],
                          n_prompt_chunks=n_prompt // chunk, n_sample_rows=n_sample)
    x, xt = _matmul_res_ln(prod, w_out_b[0].astype(BF16), x, ln_mix_g[1], ln_mix_b[1],
                           alpha=alpha, name="gmlp_out_proj_ln")
    x, xb = _peer(x, xt, peer_w_q[1], peer_sub_keys[1], exp_u_all, exp_vt_all,
                  ln_ffn_g[1], ln_ffn_b[1], alpha=alpha, layer=1)

    return (x[:n_prompt].reshape(bp, tp, d), x[n_prompt:].reshape(bs, ts, d),
            c_p[None], n_p[None], m_p[None], c_s[None], n_s[None], m_s[None],
            v_s.reshape(1, bs, ts, half))
```

```python
import functools
import math

import jax
import jax.numpy as jnp
from jax import lax
from jax.experimental import pallas as pl
from jax.experimental.pallas import tpu as pltpu

F32 = jnp.float32
BF16 = jnp.bfloat16

LN_EPS = 1e-5
PEER_TOPK = 16
MLSTM_CHUNK = 256
SAMPLE_PAD_LEN = 8
MLSTM_STAGE_MAJOR_MAX_CHUNK = 256
GATE_LANES = 128
LANES = 128
F32_SUBLANES = 8
BF16_SUBLANES = 16
NOT_RANKED = 255.0
V7X_VMEM_LIMIT_BYTES = 48 * 1024 * 1024
PEER_EXPERT_VMEM_LIMIT_BYTES = 56 * 1024 * 1024
PEER_EXPERT_ROWS = 8
SELECT_Q_SLAB_ROWS = 512
MM_SLAB = 512
MM_DEPTH = 256
NEG_BIG = -1e30
POS_BIG = 1e30


def _params(*sem):
    return pltpu.CompilerParams(dimension_semantics=sem,
                                vmem_limit_bytes=V7X_VMEM_LIMIT_BYTES)


def _pick_tile(n, candidates):
    for c in candidates:
        if n % c == 0:
            return c
    raise ValueError(f"no tile in {candidates} divides {n}")


def _layer_norm(z, g, b):
    mu = jnp.mean(z, axis=-1, keepdims=True)
    zc = z - mu
    var = jnp.mean(zc * zc, axis=-1, keepdims=True)
    return zc * lax.rsqrt(var + LN_EPS) * g + b


def _gelu_tanh(x):
    k0 = -2.0 * math.sqrt(2.0 / math.pi) * math.log2(math.e)
    k1 = 0.044715 * k0
    return x / (1.0 + jnp.exp2(x * (x * x * k1 + k0)))


def _log_sigmoid(x):
    return jnp.minimum(x, 0.0) - jnp.log1p(jnp.exp(-jnp.abs(x)))


def _mm_kernel(x_ref, w_ref, b_ref, o_ref, *, act):
    acc = jnp.dot(x_ref[...], w_ref[...], preferred_element_type=F32)
    acc = acc + b_ref[...]
    if act == "gelu":
        acc = _gelu_tanh(acc)
    o_ref[...] = acc.astype(o_ref.dtype)


def _matmul(x, w, bias, *, act=None, out_dtype=F32, name):
    m, k = x.shape
    n = w.shape[1]
    tm = _pick_tile(m, (512, 256, 128))
    tn = _pick_tile(n, (1024, 512, 256, 128))
    return pl.pallas_call(
        functools.partial(_mm_kernel, act=act),
        out_shape=jax.ShapeDtypeStruct((m, n), out_dtype),
        grid=(n // tn, m // tm),
        in_specs=[
            pl.BlockSpec((tm, k), lambda j, i: (i, 0)),
            pl.BlockSpec((k, tn), lambda j, i: (0, j)),
            pl.BlockSpec((1, tn), lambda j, i: (0, j)),
        ],
        out_specs=pl.BlockSpec((tm, tn), lambda j, i: (i, j)),
        compiler_params=_params("parallel", "parallel"),
        name=name,
    )(x, w, bias.reshape(1, n).astype(F32))


def _gate_kernel(x_ref, w_ref, b_ref, o_ref):
    x = x_ref[...]
    w = w_ref[...]
    xh = x.astype(BF16)
    xl = (x - xh.astype(F32)).astype(BF16)
    wh = w.astype(BF16)
    wl = (w - wh.astype(F32)).astype(BF16)
    acc = jnp.dot(xh, wh, preferred_element_type=F32)
    acc = acc + jnp.dot(xl, wh, preferred_element_type=F32)
    acc = acc + jnp.dot(xh, wl, preferred_element_type=F32)
    o_ref[...] = acc + b_ref[...]


def _gates(x, w_gate, b_gate):
    m, k = x.shape
    tm = _pick_tile(m, (512, 256, 128))
    return pl.pallas_call(
        _gate_kernel,
        out_shape=jax.ShapeDtypeStruct((m, GATE_LANES), F32),
        grid=(m // tm,),
        in_specs=[
            pl.BlockSpec((tm, k), lambda i: (i, 0)),
            pl.BlockSpec((k, GATE_LANES), lambda i: (0, 0)),
            pl.BlockSpec((1, GATE_LANES), lambda i: (0, 0)),
        ],
        out_specs=pl.BlockSpec((tm, GATE_LANES), lambda i: (i, 0)),
        compiler_params=_params("parallel"),
        name="mlstm_gates",
    )(x, w_gate, b_gate)


def _mlstm_kernel(q_ref, k_ref, v_ref, o_ref, g_ref, gt_ref, c0_ref, n0_ref,
                  m0_ref, gain_ref, hn_ref, c_ref, n_ref, m_ref,
                  *, heads, dk, dv, chunk, stage_major):
    @pl.when(pl.program_id(1) == 0)
    def _():
        c_ref[...] = c0_ref[...]
        n_ref[...] = n0_ref[...]
        m_ref[...] = m0_ref[...]

    g = g_ref[0]
    gt = gt_ref[0]
    row = lax.broadcasted_iota(jnp.int32, (chunk, chunk), 0)
    col = lax.broadcasted_iota(jnp.int32, (chunk, chunk), 1)
    causal = row >= col
    tril = causal.astype(F32)
    triu = (row <= col).astype(F32)
    lf_col = _log_sigmoid(g)
    lf_row = _log_sigmoid(gt)
    b_col_all = jnp.dot(tril, lf_col, precision=lax.Precision.HIGHEST,
                        preferred_element_type=F32)
    b_row_all = jnp.dot(lf_row, triu, precision=lax.Precision.HIGHEST,
                        preferred_element_type=F32)
    k_scale = dk ** -0.5

    def gate_terms(h):
        bc = b_col_all[:, heads + h:heads + h + 1]
        br = b_row_all[heads + h:heads + h + 1, :]
        ic = g[:, h:h + 1]
        ir = gt[h:h + 1, :]
        m_prev = m_ref[0, h][:, 0:1]
        dlog = jnp.where(causal, bc - br + ir, -jnp.inf)
        inter = bc + m_prev
        m_t = jnp.maximum(inter, jnp.max(dlog, axis=-1, keepdims=True))
        b_end = bc[chunk - 1:chunk, :]
        m_new = jnp.maximum(b_end + m_prev,
                            jnp.max(b_end - br + ir, axis=-1, keepdims=True))
        return dict(decay_mat=jnp.exp(dlog - m_t), w_inter=jnp.exp(inter - m_t),
                    floor=jnp.exp(-m_t), m_new=m_new,
                    w_k=jnp.exp(b_end - bc + ic - m_new),
                    decay=jnp.exp(b_end + m_prev - m_new))

    def load_qkv(h):
        qh = q_ref[0, :, h * dk:(h + 1) * dk]
        kh = k_ref[0, :, h * dk:(h + 1) * dk] * k_scale
        vb = v_ref[0, :, h * dv:(h + 1) * dv].astype(BF16)
        return qh, kh, vb

    def scores(qh, kh):
        return lax.dot_general(qh.astype(BF16), kh.astype(BF16), (((1,), (1,)), ((), ())),
                               preferred_element_type=F32)

    def read_out(h, qh, vb, qk, t):
        c_prev = c_ref[0, h]
        n_prev = n_ref[0, h]
        s = qk * t["decay_mat"]
        num = jnp.dot(s.astype(BF16), vb, preferred_element_type=F32)
        num = num + t["w_inter"] * jnp.dot(qh.astype(BF16), c_prev.astype(BF16),
                                           preferred_element_type=F32)
        den = jnp.sum(s, axis=-1, keepdims=True)
        den = den + t["w_inter"] * jnp.sum(qh * n_prev, axis=-1, keepdims=True)
        return num / jnp.maximum(jnp.abs(den), t["floor"])

    def gate_and_norm(h, hval):
        hg = hval * jax.nn.sigmoid(o_ref[0, :, h * dv:(h + 1) * dv])
        mu = jnp.mean(hg, axis=-1, keepdims=True)
        hc = hg - mu
        var = jnp.mean(hc * hc, axis=-1, keepdims=True)
        hn = hc * lax.rsqrt(var + LN_EPS) * gain_ref[h]
        hn_ref[0, :, h * dv:(h + 1) * dv] = hn.astype(hn_ref.dtype)

    def update_state(h, kh, vb, t):
        kw = kh * t["w_k"]
        c_ref[0, h] = t["decay"] * c_ref[0, h] + lax.dot_general(
            kw.astype(BF16), vb, (((0,), (0,)), ((), ())), preferred_element_type=F32)
        n_ref[0, h] = t["decay"] * n_ref[0, h] + jnp.sum(kw, axis=0, keepdims=True)
        m_ref[0, h] = jnp.broadcast_to(t["m_new"], (1, GATE_LANES))

    if stage_major:
        qkv = [load_qkv(h) for h in range(heads)]
        qks = [scores(qh, kh) for qh, kh, _ in qkv]
        terms = [gate_terms(h) for h in range(heads)]
        hvals = [read_out(h, qkv[h][0], qkv[h][2], qks[h], terms[h]) for h in range(heads)]
        for h in range(heads):
            update_state(h, qkv[h][1], qkv[h][2], terms[h])
        for h in range(heads):
            gate_and_norm(h, hvals[h])
    else:
        for h in range(heads):
            qh, kh, vb = load_qkv(h)
            t = gate_terms(h)
            gate_and_norm(h, read_out(h, qh, vb, scores(qh, kh), t))
            update_state(h, kh, vb, t)


def _mlstm(proj, gates, c0, n0, m0, gain, *, bsz, seq_len, heads, dk, dv, chunk, name):
    hk, hv = heads * dk, heads * dv
    chunks = seq_len // chunk
    if proj.shape[0] == 1:
        rows = lambda b, c: (0, b * chunks + c)
    else:
        rows = lambda b, c: (b, c)
    gt = jnp.swapaxes(gates[..., :2 * heads], 1, 2)
    n0 = n0.reshape(bsz, heads, 1, dk)
    m0 = jnp.broadcast_to(m0[..., None, None], (bsz, heads, 1, GATE_LANES))
    state_spec = lambda shape: pl.BlockSpec((1,) + shape, lambda b, c: (b, 0, 0, 0))
    tok_spec = lambda width, col: pl.BlockSpec((1, chunk, width), lambda b, c: rows(b, c) + (col,))
    hn, c, n, m = pl.pallas_call(
        functools.partial(_mlstm_kernel, heads=heads, dk=dk, dv=dv, chunk=chunk,
                          stage_major=chunk <= MLSTM_STAGE_MAJOR_MAX_CHUNK),
        out_shape=(
            jax.ShapeDtypeStruct((proj.shape[0], bsz * seq_len // proj.shape[0], hv), BF16),
            jax.ShapeDtypeStruct((bsz, heads, dk, dv), F32),
            jax.ShapeDtypeStruct((bsz, heads, 1, dk), F32),
            jax.ShapeDtypeStruct((bsz, heads, 1, GATE_LANES), F32),
        ),
        grid=(bsz, chunks),
        in_specs=[
            tok_spec(hk, 0),
            tok_spec(hk, 1),
            tok_spec(hv, 1),
            tok_spec(hv, 2),
            tok_spec(GATE_LANES, 0),
            pl.BlockSpec((1, 2 * heads, chunk),
                         lambda b, c: (rows(b, c)[0], 0, rows(b, c)[1])),
            state_spec((heads, dk, dv)),
            state_spec((heads, 1, dk)),
            state_spec((heads, 1, GATE_LANES)),
            pl.BlockSpec((heads, 1, dv), lambda b, c: (0, 0, 0)),
        ],
        out_specs=(
            tok_spec(hv, 0),
            state_spec((heads, dk, dv)),
            state_spec((heads, 1, dk)),
            state_spec((heads, 1, GATE_LANES)),
        ),
        compiler_params=_params("parallel", "arbitrary"),
        name=name,
    )(proj, proj, proj, proj, gates, gt, c0, n0, m0, gain.reshape(heads, 1, dv))
    return hn, c, n.reshape(bsz, heads, dk), m[:, :, 0, 0]


def _mm_res_ln_kernel(a_ref, w_ref, x_ref, g_ref, b_ref, o_ref, obt_ref, *, alpha):
    y = jnp.dot(a_ref[...], w_ref[...], preferred_element_type=F32)
    out = _layer_norm(alpha * x_ref[...] + y, g_ref[...], b_ref[...])
    o_ref[...] = out
    obt_ref[...] = out.T.astype(BF16)


def _matmul_res_ln(a, w, x, g, b, *, alpha, name):
    m, k = a.shape
    d = w.shape[1]
    tm = _pick_tile(m, (256, 128))
    row = lambda i: (i, 0)
    const = lambda i: (0, 0)
    return pl.pallas_call(
        functools.partial(_mm_res_ln_kernel, alpha=alpha),
        out_shape=(jax.ShapeDtypeStruct((m, d), F32),
                   jax.ShapeDtypeStruct((d, m), BF16)),
        grid=(m // tm,),
        in_specs=[
            pl.BlockSpec((tm, k), row),
            pl.BlockSpec((k, d), const, pipeline_mode=pl.Buffered(1)),
            pl.BlockSpec((tm, d), row),
            pl.BlockSpec((1, d), const),
            pl.BlockSpec((1, d), const),
        ],
        out_specs=(pl.BlockSpec((tm, d), row), pl.BlockSpec((d, tm), lambda i: (0, i))),
        compiler_params=_params("parallel"),
        name=name,
    )(a, w, x, g.reshape(1, d), b.reshape(1, d))


def _res_ln_kernel(x_ref, yt_ref, g_ref, b_ref, o_ref, ob_ref, *, alpha):
    out = _layer_norm(alpha * x_ref[...] + yt_ref[...].T, g_ref[...], b_ref[...])
    o_ref[...] = out
    ob_ref[...] = out.astype(BF16)


def _res_ln(x, yt, g, b, *, alpha, name):
    m, d = x.shape
    tm = _pick_tile(m, (256, 128))
    row = lambda i: (i, 0)
    const = lambda i: (0, 0)
    return pl.pallas_call(
        functools.partial(_res_ln_kernel, alpha=alpha),
        out_shape=(jax.ShapeDtypeStruct((m, d), F32),
                   jax.ShapeDtypeStruct((m, d), BF16)),
        grid=(m // tm,),
        in_specs=[pl.BlockSpec((tm, d), row), pl.BlockSpec((d, tm), lambda i: (0, i)),
                  pl.BlockSpec((1, d), const), pl.BlockSpec((1, d), const)],
        out_specs=(pl.BlockSpec((tm, d), row), pl.BlockSpec((tm, d), row)),
        compiler_params=_params("parallel"),
        name=name,
    )(x, yt, g.reshape(1, d), b.reshape(1, d))


def _gmlp_mix_kernel(u_ref, v_ref, wmix_ref, bmix_ref, lg_ref, lb_ref,
                     prod_ref, vln_ref, *, groups, gdim):
    vln = _layer_norm(v_ref[...], lg_ref[...], lb_ref[...])
    vln_ref[...] = vln
    for g in range(groups):
        sl = slice(g * gdim, (g + 1) * gdim)
        mixed = jnp.dot(wmix_ref[0, g].astype(BF16), vln[:, sl].astype(BF16),
                        preferred_element_type=F32)
        mixed = mixed + bmix_ref[0, :, g:g + 1]
        prod_ref[:, sl] = (u_ref[:, sl].astype(F32) * mixed).astype(prod_ref.dtype)


def _gmlp_mix(u, v, wmix, bmix, lnv_g, lnv_b, *, n_prompt_chunks, n_sample_rows):
    n, half = v.shape
    groups, chunk = wmix.shape[1], wmix.shape[2]
    gdim = half // groups
    kind = lambda i: jnp.where(i >= n_prompt_chunks, 1, 0)
    row = lambda i: (i, 0)
    const = lambda i: (0, 0)
    return pl.pallas_call(
        functools.partial(_gmlp_mix_kernel, groups=groups, gdim=gdim),
        out_shape=(jax.ShapeDtypeStruct((n, half), BF16),
                   jax.ShapeDtypeStruct((n_sample_rows, half), F32)),
        grid=(n // chunk,),
        in_specs=[
            pl.BlockSpec((chunk, half), row),
            pl.BlockSpec((chunk, half), row),
            pl.BlockSpec((1, groups, chunk, chunk), lambda i: (kind(i), 0, 0, 0)),
            pl.BlockSpec((1, chunk, groups), lambda i: (kind(i), 0, 0)),
            pl.BlockSpec((1, half), const),
            pl.BlockSpec((1, half), const),
        ],
        out_specs=(
            pl.BlockSpec((chunk, half), row),
            pl.BlockSpec((chunk, half),
                         lambda i: (jnp.maximum(i - n_prompt_chunks, 0), 0)),
        ),
        compiler_params=_params("arbitrary"),
        name="gmlp_mix",
    )(u, v, wmix, bmix, lnv_g.reshape(1, half), lnv_b.reshape(1, half))


def _sort16_network():
    def merge(lo, hi, r):
        step = 2 * r
        if step < hi - lo:
            yield from merge(lo, hi, step)
            yield from merge(lo + r, hi, step)
            for i in range(lo + r, hi - r, step):
                yield (i, i + r)
        else:
            yield (lo, lo + r)

    def sort(lo, hi):
        if hi > lo:
            mid = lo + (hi - lo) // 2
            yield from sort(lo, mid)
            yield from sort(mid + 1, hi)
            yield from merge(lo, hi, 1)

    return tuple(sort(0, PEER_TOPK - 1))


def _top16_sorted(s):
    n = PEER_TOPK
    assert s.shape[0] == n * F32_SUBLANES
    v = [s[k * F32_SUBLANES:(k + 1) * F32_SUBLANES, :] for k in range(n)]

    def exchange(i, j):
        v[i], v[j] = jnp.maximum(v[i], v[j]), jnp.minimum(v[i], v[j])

    for i, j in _sort16_network():
        exchange(i, j)
    shift = F32_SUBLANES // 2
    while shift >= 1:
        other = [pltpu.roll(x, shift, 0) for x in v]
        v = [jnp.maximum(v[k], other[n - 1 - k]) for k in range(n)]
        d = n // 2
        while d >= 1:
            for k in range(n):
                if k & d == 0:
                    exchange(k, k + d)
            d //= 2
        shift //= 2
    return v


def _peer_select_kernel(xt_ref, wq_ref, sk_ref, cnt0_ref, pz0_ref, rank1_ref, p1_ref,
                        qt_ref, s0_ref, tops_ref, *, heads, nkeys):
    tm = qt_ref.shape[1]
    slab_heads = max(1, SELECT_Q_SLAB_ROWS // (2 * nkeys))
    slab_rows = slab_heads * 2 * nkeys

    def project(first_head):
        r = slice(first_head * 2 * nkeys, first_head * 2 * nkeys + slab_rows)
        qt_ref[r, :] = lax.dot_general(wq_ref[:, r], xt_ref[...], (((0,), (0,)), ((), ())),
                                       preferred_element_type=F32)

    project(0)
    for h in range(heads):
        if h % slab_heads == 0 and h + slab_heads < heads:
            project(h + slab_heads)
        for p in range(2):
            r0 = (2 * h + p) * nkeys
            qhp = qt_ref[r0:r0 + nkeys, :].astype(BF16)
            s = jnp.dot(sk_ref[h, p], qhp, preferred_element_type=F32)
            tops = _top16_sorted(s)
            for r in range(PEER_TOPK):
                tops_ref[p, r, h:h + 1, :] = tops[r][0:1, :]
            if p == 0:
                s0_ref[h] = s
            else:
                slabs = []
                for k in range(nkeys // F32_SUBLANES):
                    sk_slab = s[k * F32_SUBLANES:(k + 1) * F32_SUBLANES, :]
                    rank = jnp.zeros(sk_slab.shape, F32)
                    for l in range(PEER_TOPK):
                        rank = jnp.where(sk_slab < tops[l], float(l + 1), rank)
                    slabs.append(rank)
                rank = jnp.concatenate(slabs, axis=0)
                rank1_ref[h] = pltpu.bitcast(rank.astype(BF16), jnp.uint32)
                p1_ref[h] = pltpu.bitcast(
                    jnp.exp(s - tops_ref[1, 0, h:h + 1, :]).astype(BF16), jnp.uint32)

    a = [tops_ref[0, r] for r in range(PEER_TOPK)]
    b = [tops_ref[1, r] for r in range(PEER_TOPK)]
    cands = [a[k] + b[l] for k in range(PEER_TOPK) for l in range(PEER_TOPK)
             if (k + 1) * (l + 1) <= PEER_TOPK]
    cur = jnp.full(cands[0].shape, jnp.inf, F32)
    for _ in range(PEER_TOPK):
        nxt = jnp.full(cands[0].shape, -jnp.inf, F32)
        for c in cands:
            nxt = jnp.maximum(nxt, jnp.where(c < cur, c, -jnp.inf))
        cur = nxt
    thr = cur
    top = cands[0]
    z = jnp.zeros(thr.shape, F32)
    for c in cands:
        z = z + jnp.where(c >= thr, jnp.exp(c - top), 0.0)
    inv_z = 1.0 / z
    for h in range(heads):
        s0 = s0_ref[h]
        thr_h = thr[h:h + 1, :]
        b_h = [b[l][h:h + 1, :] for l in range(PEER_TOPK)]
        cnt = jnp.zeros(s0.shape, F32)
        step = PEER_TOPK // 2
        while step >= 1:
            probe = b_h[step - 1]
            for base in range(2 * step, PEER_TOPK, 2 * step):
                probe = jnp.where(cnt == float(base), b_h[base + step - 1], probe)
            cnt = cnt + jnp.where(s0 + probe >= thr_h, float(step), 0.0)
            step //= 2
        cnt = jnp.where(s0 + b_h[PEER_TOPK - 1] >= thr_h, float(PEER_TOPK), cnt)
        cnt0_ref[h] = cnt
        pz0_ref[h] = jnp.exp(s0 - a[0][h:h + 1, :]) * inv_z[h:h + 1, :]


def _peer_select(xt, wq, sk):
    d, n = xt.shape
    heads, _, nkeys, _ = sk.shape
    tm = _pick_tile(n, (256, 128))
    key_spec = pl.BlockSpec((heads, nkeys, tm), lambda t: (0, 0, t))
    pair_spec = pl.BlockSpec((heads, nkeys // 2, tm), lambda t: (0, 0, t))
    return pl.pallas_call(
        functools.partial(_peer_select_kernel, heads=heads, nkeys=nkeys),
        out_shape=(jax.ShapeDtypeStruct((heads, nkeys, n), F32),
                   jax.ShapeDtypeStruct((heads, nkeys, n), F32),
                   jax.ShapeDtypeStruct((heads, nkeys // 2, n), jnp.uint32),
                   jax.ShapeDtypeStruct((heads, nkeys // 2, n), jnp.uint32)),
        grid=(n // tm,),
        in_specs=[
            pl.BlockSpec((d, tm), lambda t: (0, t)),
            pl.BlockSpec(wq.shape, lambda t: (0, 0), pipeline_mode=pl.Buffered(1)),
            pl.BlockSpec(sk.shape, lambda t: (0, 0, 0, 0)),
        ],
        out_specs=(key_spec, key_spec, pair_spec, pair_spec),
        scratch_shapes=[pltpu.VMEM((wq.shape[1], tm), F32),
                        pltpu.VMEM((heads, nkeys, tm), F32),
                        pltpu.VMEM((2, PEER_TOPK, heads, tm), F32)],
        compiler_params=_params("parallel"),
        name="peer_select",
    )(xt, wq, sk)


def _peer_expert_kernel(xt_ref, u_ref, v_ref, cnt0_ref, pz0_ref, rank1_ref, p1_ref,
                        out_ref, at0_ref, at1_ref, wt0_ref, wt1_ref, cnt_s, pz_s,
                        *, heads, nkeys, n_blocks, n_items):
    s = pl.program_id(0)
    eb, tm = at0_ref.shape
    rows = eb // nkeys

    @pl.when(s == 0)
    def _():
        at1_ref[...] = jnp.zeros_like(at1_ref)
        wt0_ref[...] = jnp.zeros_like(wt0_ref)

    @pl.when((s <= 2) | (lax.rem(s - 2, n_blocks) == 0))
    def _():
        out_ref[...] = jnp.zeros_like(out_ref)

    item_b = jnp.clip(s - 1, 0, n_items - 1)
    valid_b = jnp.where((s >= 1) & (s <= n_items), 1.0, 0.0).astype(F32)
    first_row = pl.multiple_of(lax.rem(item_b, n_blocks) * rows, rows)

    d_model = out_ref.shape[0]
    reps = nkeys // BF16_SUBLANES
    lane_tiles = tm // LANES

    def routing_piece(at_r, wt_w, ii, tc):
        keys = slice(ii * nkeys, (ii + 1) * nkeys)
        lanes = slice(tc * LANES, (tc + 1) * LANES)
        g = jnp.zeros((nkeys, LANES), BF16)
        for h in range(heads):
            cnt = pltpu.bitcast(pltpu.repeat(cnt_s[h, ii, :, lanes], reps, axis=0), BF16)
            pz = pltpu.bitcast(pltpu.repeat(pz_s[h, ii, :, lanes], reps, axis=0), BF16)
            r1 = pltpu.bitcast(rank1_ref[h, :, lanes], BF16)
            p1 = pltpu.bitcast(p1_ref[h, :, lanes], BF16)
            g = g + jnp.where(r1 < cnt, p1 * pz, jnp.zeros_like(p1))
        wt_w[keys, lanes] = (_gelu_tanh(at_r[keys, lanes]) * g.astype(F32)).astype(BF16)

    def stage_a_piece(at_w, slab, kc):
        r = slice(slab * MM_SLAB, (slab + 1) * MM_SLAB)
        k = slice(kc * MM_DEPTH, (kc + 1) * MM_DEPTH)
        part = jnp.dot(u_ref[r, k], xt_ref[k, :], preferred_element_type=F32)
        if kc == 0:
            at_w[r, :] = part
        else:
            at_w[r, :] += part

    def stage_c_piece(wt_r, slab, kc):
        r = slice(slab * MM_SLAB, (slab + 1) * MM_SLAB)
        k = slice(kc * MM_DEPTH, (kc + 1) * MM_DEPTH)
        out_ref[r, :] += lax.dot_general(v_ref[k, r], wt_r[k, :], (((0,), (0,)), ((), ())),
                                         preferred_element_type=F32)

    def step_interleaved(at_w, at_r, wt_w, wt_r):
        stage_rows(first_row)
        b_pieces = [(ii, tc) for ii in range(rows) for tc in range(lane_tiles)]
        a_pieces = [(slab, kc) for slab in range(eb // MM_SLAB)
                    for kc in range(d_model // MM_DEPTH)]
        c_pieces = [(slab, kc) for slab in range(d_model // MM_SLAB)
                    for kc in range(eb // MM_DEPTH)]
        mxu_pieces = []
        for i in range(max(len(a_pieces), len(c_pieces))):
            if i < len(c_pieces):
                mxu_pieces.append(("c", c_pieces[i]))
            if i < len(a_pieces):
                mxu_pieces.append(("a", a_pieces[i]))
        n_slots = max(len(b_pieces), len(mxu_pieces))
        for slot in range(n_slots):
            for kind, (slab, kc) in mxu_pieces[slot * len(mxu_pieces) // n_slots:
                                               (slot + 1) * len(mxu_pieces) // n_slots]:
                if kind == "a":
                    stage_a_piece(at_w, slab, kc)
                else:
                    stage_c_piece(wt_r, slab, kc)
            for ii, tc in b_pieces[slot * len(b_pieces) // n_slots:
                                   (slot + 1) * len(b_pieces) // n_slots]:
                routing_piece(at_r, wt_w, ii, tc)

    def stage_rows(first_row):
        for h in range(heads):
            cnt_rows = cnt0_ref[h, pl.ds(first_row, rows), :]
            pz_rows = pz0_ref[h, pl.ds(first_row, rows), :] * valid_b
            for ii in range(rows):
                cnt_s[h, ii] = pltpu.bitcast(
                    jnp.broadcast_to(cnt_rows[ii:ii + 1, :], (BF16_SUBLANES, tm)).astype(BF16),
                    jnp.uint32)
                pz_s[h, ii] = pltpu.bitcast(
                    jnp.broadcast_to(pz_rows[ii:ii + 1, :], (BF16_SUBLANES, tm)).astype(BF16),
                    jnp.uint32)

    @pl.when(lax.rem(s, 2) == 0)
    def _():
        step_interleaved(at0_ref, at1_ref, wt1_ref, wt0_ref)

    @pl.when(lax.rem(s, 2) == 1)
    def _():
        step_interleaved(at1_ref, at0_ref, wt0_ref, wt1_ref)


def _peer_expert(xt, u, vt, cnt0, pz0, rank1, p1, *, layer):
    d, n = xt.shape
    n_exp = u.shape[1]
    heads, nkeys, _ = cnt0.shape
    tm = _pick_tile(n, (512, 256, 128))
    eb = PEER_EXPERT_ROWS * nkeys
    n_blocks = n_exp // eb
    n_items = (n // tm) * n_blocks

    def item(s, lag):
        return jnp.clip(s - lag, 0, n_items - 1)

    tile_b = lambda s: item(s, 1) // n_blocks
    row_spec = pl.BlockSpec((heads, nkeys, tm), lambda s: (0, 0, tile_b(s)))
    pair_spec = pl.BlockSpec((heads, nkeys // 2, tm), lambda s: (0, 0, tile_b(s)))
    return pl.pallas_call(
        functools.partial(_peer_expert_kernel, heads=heads, nkeys=nkeys,
                          n_blocks=n_blocks, n_items=n_items),
        out_shape=jax.ShapeDtypeStruct((d, n), F32),
        grid=(n_items + 2,),
        in_specs=[
            pl.BlockSpec((d, tm), lambda s: (0, item(s, 0) // n_blocks)),
            pl.BlockSpec((None, eb, d), lambda s: (layer, item(s, 0) % n_blocks, 0)),
            pl.BlockSpec((None, eb, d), lambda s: (layer, item(s, 2) % n_blocks, 0)),
            row_spec, row_spec, pair_spec, pair_spec,
        ],
        out_specs=pl.BlockSpec((d, tm), lambda s: (0, item(s, 2) // n_blocks)),
        scratch_shapes=[pltpu.VMEM((eb, tm), F32), pltpu.VMEM((eb, tm), F32),
                        pltpu.VMEM((eb, tm), BF16), pltpu.VMEM((eb, tm), BF16),
                        pltpu.VMEM((heads, PEER_EXPERT_ROWS, F32_SUBLANES, tm), jnp.uint32),
                        pltpu.VMEM((heads, PEER_EXPERT_ROWS, F32_SUBLANES, tm), jnp.uint32)],
        compiler_params=pltpu.CompilerParams(
            dimension_semantics=("arbitrary",), vmem_limit_bytes=PEER_EXPERT_VMEM_LIMIT_BYTES),
        name="peer_expert",
    )(xt, u, vt, cnt0, pz0, rank1, p1)


def _peer(x, xt, w_q, sub_keys, exp_u_all, exp_vt_all, ln_g, ln_b, *, alpha, layer):
    sk = sub_keys.astype(BF16)
    cnt0, pz0, rank1, p1 = _peer_select(xt, w_q.astype(BF16), sk)
    out_t = _peer_expert(xt, exp_u_all, exp_vt_all, cnt0, pz0, rank1, p1, layer=layer)
    return _res_ln(x, out_t, ln_g, ln_b, alpha=alpha, name=f"peer_ln_{layer}")


def kernel(x_prompt, x_sample, state_mlstm_C, state_mlstm_n, state_mlstm_m,
           w_in_a, b_gate_a, hn_gain_a, w_out_a, w_in_b, b_in_b, lnv_g_b, lnv_b_b,
           w_s_b, b_s_b, w_out_b, ln_mix_g, ln_mix_b, ln_ffn_g, ln_ffn_b,
           peer_w_q, peer_sub_keys, peer_u, peer_v):
    bp, tp, d = x_prompt.shape
    bs, ts, _ = x_sample.shape
    depth = ln_mix_g.shape[0]
    alpha = float((2 * depth) ** 0.25)
    heads = b_gate_a.shape[-1] // 2
    dv = hn_gain_a.shape[-1]
    dk = state_mlstm_n.shape[-1]
    hk, hv = heads * dk, heads * dv
    n_prompt, n_sample = bp * tp, bs * ts
    assert depth == 2 and w_in_a.shape[0] == 1 and w_in_b.shape[0] == 1
    assert ts <= SAMPLE_PAD_LEN and tp % MLSTM_CHUNK == 0

    x = jnp.concatenate([x_prompt.reshape(n_prompt, d), x_sample.reshape(n_sample, d)])
    xb = x.astype(BF16)

    n_main = 2 * hk + 2 * hv
    proj = _matmul(xb, w_in_a[0][:, :n_main].astype(BF16), jnp.zeros((n_main,), F32),
                   name="mlstm_in_proj")
    w_gate = jnp.pad(w_in_a[0][:, n_main:], ((0, 0), (0, GATE_LANES - 2 * heads)))
    b_gate = jnp.pad(b_gate_a[0], (0, GATE_LANES - 2 * heads)).reshape(1, GATE_LANES)
    gates = _gates(x, w_gate, b_gate)

    zeros_c = jnp.zeros((bp, heads, dk, dv), F32)
    hn_p, c_p, n_p, m_p = _mlstm(
        proj[None], gates[None],
        zeros_c, jnp.zeros((bp, heads, dk), F32), jnp.zeros((bp, heads), F32),
        hn_gain_a[0], bsz=bp, seq_len=tp, heads=heads, dk=dk, dv=dv, chunk=MLSTM_CHUNK,
        name="mlstm_prompt")

    pad_t = SAMPLE_PAD_LEN - ts
    proj_s = jnp.pad(proj[n_prompt:].reshape(bs, ts, n_main), ((0, 0), (0, pad_t), (0, 0)))
    lane = jnp.arange(GATE_LANES)
    pad_gate = jnp.where(lane < heads, NEG_BIG, jnp.where(lane < 2 * heads, POS_BIG, 0.0))
    gates_s = jnp.concatenate(
        [gates[n_prompt:].reshape(bs, ts, GATE_LANES),
         jnp.broadcast_to(pad_gate.astype(F32), (bs, pad_t, GATE_LANES))], axis=1)
    hn_s, c_s, n_s, m_s = _mlstm(
        proj_s, gates_s, state_mlstm_C[0], state_mlstm_n[0], state_mlstm_m[0],
        hn_gain_a[0], bsz=bs, seq_len=SAMPLE_PAD_LEN, heads=heads, dk=dk, dv=dv,
        chunk=SAMPLE_PAD_LEN, name="mlstm_sample")

    hn = jnp.concatenate([hn_p[0], hn_s[:, :ts].reshape(n_sample, hv)])
    x, xt = _matmul_res_ln(hn, w_out_a[0].astype(BF16), x, ln_mix_g[0], ln_mix_b[0],
                           alpha=alpha, name="mlstm_out_proj_ln")
    exp_u_all = peer_u.astype(BF16)
    exp_vt_all = peer_v.astype(BF16)
    x, xb = _peer(x, xt, peer_w_q[0], peer_sub_keys[0], exp_u_all, exp_vt_all,
                  ln_ffn_g[0], ln_ffn_b[0], alpha=alpha, layer=0)

    half = w_in_b.shape[-1] // 2
    groups, chunk = w_s_b.shape[1], w_s_b.shape[2]
    assert tp % chunk == 0 and n_sample % chunk == 0 and chunk % ts == 0
    w_in = w_in_b[0].astype(BF16)
    u = _matmul(xb, w_in[:, :half], b_in_b[0][:half], act="gelu", out_dtype=BF16,
                name="gmlp_in_proj_u")
    v = _matmul(xb, w_in[:, half:], b_in_b[0][half:], act="gelu", name="gmlp_in_proj_v")
    ws = jnp.where(jnp.tril(jnp.ones((chunk, chunk), dtype=bool)), w_s_b[0], 0.0)
    ws_sample = jnp.einsum("ab,gts->gatbs", jnp.eye(chunk // ts, dtype=F32),
                           ws[:, :ts, :ts]).reshape(groups, chunk, chunk)
    wmix = jnp.stack([ws, ws_sample])
    bmix = jnp.stack([b_s_b[0].T, jnp.tile(b_s_b[0][:, :ts].T, (chunk // ts, 1))])
    prod, v_s = _gmlp_mix(u, v, wmix, bmix, lnv_g_b[0], lnv_b_b[0],
                          n_prompt_chunks=n_prompt // chunk, n_sample_rows=n_sample)
    x, xt = _matmul_res_ln(prod, w_out_b[0].astype(BF16), x, ln_mix_g[1], ln_mix_b[1],
                           alpha=alpha, name="gmlp_out_proj_ln")
    x, xb = _peer(x, xt, peer_w_q[1], peer_sub_keys[1], exp_u_all, exp_vt_all,
                  ln_ffn_g[1], ln_ffn_b[1], alpha=alpha, layer=1)

    return (x[:n_prompt].reshape(bp, tp, d), x[n_prompt:].reshape(bs, ts, d),
            c_p[None], n_p[None], m_p[None], c_s[None], n_s[None], m_s[None],
            v_s.reshape(1, bs, ts, half))
```

```python
import functools
import math

import jax
import jax.numpy as jnp
from jax import lax
from jax.experimental import pallas as pl
from jax.experimental.pallas import tpu as pltpu

F32 = jnp.float32
BF16 = jnp.bfloat16

LN_EPS = 1e-5
PEER_TOPK = 16
MLSTM_CHUNK = 256
SAMPLE_PAD_LEN = 8
GATE_LANES = 128
LANES = 128
F32_SUBLANES = 8
BF16_SUBLANES = 16
V7X_VMEM_LIMIT_BYTES = 48 * 1024 * 1024
PEER_EXPERT_VMEM_LIMIT_BYTES = 56 * 1024 * 1024
PEER_EXPERT_ROWS = 8
SELECT_Q_SLAB_ROWS = 512
MM_SLAB = 512
MM_DEPTH = 256
NEG_BIG = -1e30
POS_BIG = 1e30


def _params(*sem):
    return pltpu.CompilerParams(dimension_semantics=sem,
                                vmem_limit_bytes=V7X_VMEM_LIMIT_BYTES)


def _pick_tile(n, candidates):
    for c in candidates:
        if n % c == 0:
            return c
    raise ValueError(f"no tile in {candidates} divides {n}")


def _layer_norm(z, g, b):
    mu = jnp.mean(z, axis=-1, keepdims=True)
    zc = z - mu
    var = jnp.mean(zc * zc, axis=-1, keepdims=True)
    return zc * lax.rsqrt(var + LN_EPS) * g + b


def _gelu_tanh(x):
    k0 = -2.0 * math.sqrt(2.0 / math.pi) * math.log2(math.e)
    k1 = 0.044715 * k0
    return x / (1.0 + jnp.exp2(x * (x * x * k1 + k0)))


def _log_sigmoid(x):
    return jnp.minimum(x, 0.0) - jnp.log1p(jnp.exp(-jnp.abs(x)))


def _mm_kernel(x_ref, w_ref, b_ref, o_ref, *, act):
    acc = jnp.dot(x_ref[...], w_ref[...], preferred_element_type=F32)
    acc = acc + b_ref[...]
    if act == "gelu":
        acc = _gelu_tanh(acc)
    o_ref[...] = acc.astype(o_ref.dtype)


def _matmul(x, w, bias, *, col_start=0, n=None, act=None, out_dtype=F32, name):
    m, k = x.shape
    n = w.shape[1] - col_start if n is None else n
    tm = _pick_tile(m, (512, 256, 128))
    tn = _pick_tile(math.gcd(n, col_start) if col_start else n, (1024, 512, 256, 128))
    first = col_start // tn
    return pl.pallas_call(
        functools.partial(_mm_kernel, act=act),
        out_shape=jax.ShapeDtypeStruct((m, n), out_dtype),
        grid=(n // tn, m // tm),
        in_specs=[
            pl.BlockSpec((tm, k), lambda j, i: (i, 0)),
            pl.BlockSpec((k, tn), lambda j, i: (0, first + j)),
            pl.BlockSpec((1, tn), lambda j, i: (0, first + j)),
        ],
        out_specs=pl.BlockSpec((tm, tn), lambda j, i: (i, j)),
        compiler_params=_params("parallel", "parallel"),
        name=name,
    )(x, w, bias.reshape(1, w.shape[1]).astype(F32))


def _gate_kernel(x_ref, w_ref, b_ref, o_ref):
    x = x_ref[...]
    w = w_ref[...]
    xh = x.astype(BF16)
    xl = (x - xh.astype(F32)).astype(BF16)
    wh = w.astype(BF16)
    wl = (w - wh.astype(F32)).astype(BF16)
    acc = jnp.dot(xh, wh, preferred_element_type=F32)
    acc = acc + jnp.dot(xl, wh, preferred_element_type=F32)
    acc = acc + jnp.dot(xh, wl, preferred_element_type=F32)
    o_ref[...] = acc + b_ref[...]


def _gates(x, w_gate, b_gate):
    m, k = x.shape
    tm = _pick_tile(m, (512, 256, 128))
    return pl.pallas_call(
        _gate_kernel,
        out_shape=jax.ShapeDtypeStruct((m, GATE_LANES), F32),
        grid=(m // tm,),
        in_specs=[
            pl.BlockSpec((tm, k), lambda i: (i, 0)),
            pl.BlockSpec((k, GATE_LANES), lambda i: (0, 0)),
            pl.BlockSpec((1, GATE_LANES), lambda i: (0, 0)),
        ],
        out_specs=pl.BlockSpec((tm, GATE_LANES), lambda i: (i, 0)),
        compiler_params=_params("parallel"),
        name="mlstm_gates",
    )(x, w_gate, b_gate)


def _mlstm_kernel(q_ref, k_ref, v_ref, o_ref, g_ref, gt_ref, c0_ref, n0_ref,
                  m0_ref, gain_ref, hn_ref, c_ref, n_ref, m_ref,
                  *, heads, dk, dv, chunk):
    @pl.when(pl.program_id(1) == 0)
    def _():
        c_ref[...] = c0_ref[...]
        n_ref[...] = n0_ref[...]
        m_ref[...] = m0_ref[...]

    n_seq = g_ref.shape[0]
    row = lax.broadcasted_iota(jnp.int32, (chunk, chunk), 0)
    col = lax.broadcasted_iota(jnp.int32, (chunk, chunk), 1)
    causal = row >= col
    tril = causal.astype(F32)
    triu = (row <= col).astype(F32)
    g = [g_ref[b] for b in range(n_seq)]
    gt = [gt_ref[b] for b in range(n_seq)]
    b_col_all = [jnp.dot(tril, _log_sigmoid(x), precision=lax.Precision.HIGHEST,
                         preferred_element_type=F32) for x in g]
    b_row_all = [jnp.dot(_log_sigmoid(x), triu, precision=lax.Precision.HIGHEST,
                         preferred_element_type=F32) for x in gt]
    k_scale = dk ** -0.5

    def gate_terms(b, h):
        bc = b_col_all[b][:, heads + h:heads + h + 1]
        br = b_row_all[b][heads + h:heads + h + 1, :]
        ic = g[b][:, h:h + 1]
        ir = gt[b][h:h + 1, :]
        m_prev = m_ref[b, h][:, 0:1]
        dlog = jnp.where(causal, bc - br + ir, -jnp.inf)
        inter = bc + m_prev
        m_t = jnp.maximum(inter, jnp.max(dlog, axis=-1, keepdims=True))
        b_end = bc[chunk - 1:chunk, :]
        m_new = jnp.maximum(b_end + m_prev,
                            jnp.max(b_end - br + ir, axis=-1, keepdims=True))
        return dict(decay_mat=jnp.exp(dlog - m_t), w_inter=jnp.exp(inter - m_t),
                    floor=jnp.exp(-m_t), m_new=m_new,
                    w_k=jnp.exp(b_end - bc + ic - m_new),
                    decay=jnp.exp(b_end + m_prev - m_new))

    def load_qkv(b, h):
        qh = q_ref[b, :, h * dk:(h + 1) * dk]
        kh = k_ref[b, :, h * dk:(h + 1) * dk] * k_scale
        vb = v_ref[b, :, h * dv:(h + 1) * dv].astype(BF16)
        return qh, kh, vb

    def scores(qh, kh):
        return lax.dot_general(qh.astype(BF16), kh.astype(BF16), (((1,), (1,)), ((), ())),
                               preferred_element_type=F32)

    def read_out(b, h, qh, vb, qk, t):
        c_prev = c_ref[b, h]
        n_prev = n_ref[b, h]
        s = qk * t["decay_mat"]
        num = jnp.dot(s.astype(BF16), vb, preferred_element_type=F32)
        num = num + t["w_inter"] * jnp.dot(qh.astype(BF16), c_prev.astype(BF16),
                                           preferred_element_type=F32)
        den = jnp.sum(s, axis=-1, keepdims=True)
        den = den + t["w_inter"] * jnp.sum(qh * n_prev, axis=-1, keepdims=True)
        return num / jnp.maximum(jnp.abs(den), t["floor"])

    def gate_and_norm(b, h, hval):
        hg = hval * jax.nn.sigmoid(o_ref[b, :, h * dv:(h + 1) * dv])
        mu = jnp.mean(hg, axis=-1, keepdims=True)
        hc = hg - mu
        var = jnp.mean(hc * hc, axis=-1, keepdims=True)
        hn = hc * lax.rsqrt(var + LN_EPS) * gain_ref[h]
        hn_ref[b, :, h * dv:(h + 1) * dv] = hn.astype(hn_ref.dtype)

    def update_state(b, h, kh, vb, t):
        kw = kh * t["w_k"]
        c_ref[b, h] = t["decay"] * c_ref[b, h] + lax.dot_general(
            kw.astype(BF16), vb, (((0,), (0,)), ((), ())), preferred_element_type=F32)
        n_ref[b, h] = t["decay"] * n_ref[b, h] + jnp.sum(kw, axis=0, keepdims=True)
        m_ref[b, h] = jnp.broadcast_to(t["m_new"], (1, GATE_LANES))

    units = [(b, h) for b in range(n_seq) for h in range(heads)]
    qkv = [load_qkv(b, h) for b, h in units]
    qks = [scores(qh, kh) for qh, kh, _ in qkv]
    terms = [gate_terms(b, h) for b, h in units]
    hvals = [read_out(b, h, qkv[i][0], qkv[i][2], qks[i], terms[i])
             for i, (b, h) in enumerate(units)]
    for i, (b, h) in enumerate(units):
        update_state(b, h, qkv[i][1], qkv[i][2], terms[i])
    for i, (b, h) in enumerate(units):
        gate_and_norm(b, h, hvals[i])


def _mlstm(proj, gates, c0, n0, m0, gain, *, bsz, seq_len, heads, dk, dv, chunk,
           seqs_per_step=1, name):
    hk, hv = heads * dk, heads * dv
    chunks = seq_len // chunk
    nb = seqs_per_step
    if proj.shape[0] == 1:
        assert nb == 1
        rows = lambda b, c: (0, b * chunks + c)
    else:
        rows = lambda b, c: (b, c)
    gt = jnp.swapaxes(gates[..., :2 * heads], 1, 2)
    n0 = n0.reshape(bsz, heads, 1, dk)
    m0 = jnp.broadcast_to(m0[..., None, None], (bsz, heads, 1, GATE_LANES))
    state_spec = lambda shape: pl.BlockSpec((nb,) + shape, lambda b, c: (b, 0, 0, 0))
    tok_spec = lambda width, col: pl.BlockSpec((nb, chunk, width),
                                               lambda b, c: rows(b, c) + (col,))
    hn, c, n, m = pl.pallas_call(
        functools.partial(_mlstm_kernel, heads=heads, dk=dk, dv=dv, chunk=chunk),
        out_shape=(
            jax.ShapeDtypeStruct((proj.shape[0], bsz * seq_len // proj.shape[0], hv), BF16),
            jax.ShapeDtypeStruct((bsz, heads, dk, dv), F32),
            jax.ShapeDtypeStruct((bsz, heads, 1, dk), F32),
            jax.ShapeDtypeStruct((bsz, heads, 1, GATE_LANES), F32),
        ),
        grid=(bsz // nb, chunks),
        in_specs=[
            tok_spec(hk, 0),
            tok_spec(hk, 1),
            tok_spec(hv, 1),
            tok_spec(hv, 2),
            tok_spec(GATE_LANES, 0),
            pl.BlockSpec((nb, 2 * heads, chunk),
                         lambda b, c: (rows(b, c)[0], 0, rows(b, c)[1])),
            state_spec((heads, dk, dv)),
            state_spec((heads, 1, dk)),
            state_spec((heads, 1, GATE_LANES)),
            pl.BlockSpec((heads, 1, dv), lambda b, c: (0, 0, 0)),
        ],
        out_specs=(
            tok_spec(hv, 0),
            state_spec((heads, dk, dv)),
            state_spec((heads, 1, dk)),
            state_spec((heads, 1, GATE_LANES)),
        ),
        compiler_params=_params("parallel", "arbitrary"),
        name=name,
    )(proj, proj, proj, proj, gates, gt, c0, n0, m0, gain.reshape(heads, 1, dv))
    return hn, c, n.reshape(bsz, heads, dk), m[:, :, 0, 0]


def _mm_res_ln_kernel(a_ref, w_ref, x_ref, g_ref, b_ref, o_ref, obt_ref, *, alpha):
    y = jnp.dot(a_ref[...], w_ref[...], preferred_element_type=F32)
    out = _layer_norm(alpha * x_ref[...] + y, g_ref[...], b_ref[...])
    o_ref[...] = out
    obt_ref[...] = out.T.astype(BF16)


def _matmul_res_ln(a, w, x, g, b, *, alpha, name):
    m, k = a.shape
    d = w.shape[1]
    tm = _pick_tile(m, (256, 128))
    row = lambda i: (i, 0)
    const = lambda i: (0, 0)
    return pl.pallas_call(
        functools.partial(_mm_res_ln_kernel, alpha=alpha),
        out_shape=(jax.ShapeDtypeStruct((m, d), F32),
                   jax.ShapeDtypeStruct((d, m), BF16)),
        grid=(m // tm,),
        in_specs=[
            pl.BlockSpec((tm, k), row),
            pl.BlockSpec((k, d), const, pipeline_mode=pl.Buffered(1)),
            pl.BlockSpec((tm, d), row),
            pl.BlockSpec((1, d), const),
            pl.BlockSpec((1, d), const),
        ],
        out_specs=(pl.BlockSpec((tm, d), row), pl.BlockSpec((d, tm), lambda i: (0, i))),
        compiler_params=_params("parallel"),
        name=name,
    )(a, w, x, g.reshape(1, d), b.reshape(1, d))


def _res_ln_kernel(x_ref, yt_ref, g_ref, b_ref, o_ref, ob_ref, *, alpha):
    out = _layer_norm(alpha * x_ref[...] + yt_ref[...].T, g_ref[...], b_ref[...])
    o_ref[...] = out
    ob_ref[...] = out.astype(BF16)


def _res_ln(x, yt, g, b, *, alpha, name):
    m, d = x.shape
    tm = _pick_tile(m, (256, 128))
    row = lambda i: (i, 0)
    const = lambda i: (0, 0)
    return pl.pallas_call(
        functools.partial(_res_ln_kernel, alpha=alpha),
        out_shape=(jax.ShapeDtypeStruct((m, d), F32),
                   jax.ShapeDtypeStruct((m, d), BF16)),
        grid=(m // tm,),
        in_specs=[pl.BlockSpec((tm, d), row), pl.BlockSpec((d, tm), lambda i: (0, i)),
                  pl.BlockSpec((1, d), const), pl.BlockSpec((1, d), const)],
        out_specs=(pl.BlockSpec((tm, d), row), pl.BlockSpec((tm, d), row)),
        compiler_params=_params("parallel"),
        name=name,
    )(x, yt, g.reshape(1, d), b.reshape(1, d))


def _gmlp_mix_kernel(u_ref, v_ref, wmix_ref, bmix_ref, lg_ref, lb_ref,
                     prod_ref, vln_ref, *, groups, gdim):
    vln = _layer_norm(v_ref[...], lg_ref[...], lb_ref[...])
    vln_ref[...] = vln
    for g in range(groups):
        sl = slice(g * gdim, (g + 1) * gdim)
        mixed = jnp.dot(wmix_ref[0, g].astype(BF16), vln[:, sl].astype(BF16),
                        preferred_element_type=F32)
        mixed = mixed + bmix_ref[0, :, g:g + 1]
        prod_ref[:, sl] = (u_ref[:, sl].astype(F32) * mixed).astype(prod_ref.dtype)


def _gmlp_mix(u, v, wmix, bmix, lnv_g, lnv_b, *, n_prompt_chunks, n_sample_rows):
    n, half = v.shape
    groups, chunk = wmix.shape[1], wmix.shape[2]
    gdim = half // groups
    kind = lambda i: jnp.where(i >= n_prompt_chunks, 1, 0)
    row = lambda i: (i, 0)
    const = lambda i: (0, 0)
    return pl.pallas_call(
        functools.partial(_gmlp_mix_kernel, groups=groups, gdim=gdim),
        out_shape=(jax.ShapeDtypeStruct((n, half), BF16),
                   jax.ShapeDtypeStruct((n_sample_rows, half), F32)),
        grid=(n // chunk,),
        in_specs=[
            pl.BlockSpec((chunk, half), row),
            pl.BlockSpec((chunk, half), row),
            pl.BlockSpec((1, groups, chunk, chunk), lambda i: (kind(i), 0, 0, 0)),
            pl.BlockSpec((1, chunk, groups), lambda i: (kind(i), 0, 0)),
            pl.BlockSpec((1, half), const),
            pl.BlockSpec((1, half), const),
        ],
        out_specs=(
            pl.BlockSpec((chunk, half), row),
            pl.BlockSpec((chunk, half),
                         lambda i: (jnp.maximum(i - n_prompt_chunks, 0), 0)),
        ),
        compiler_params=_params("arbitrary"),
        name="gmlp_mix",
    )(u, v, wmix, bmix, lnv_g.reshape(1, half), lnv_b.reshape(1, half))


def _sort16_network():
    def merge(lo, hi, r):
        step = 2 * r
        if step < hi - lo:
            yield from merge(lo, hi, step)
            yield from merge(lo + r, hi, step)
            for i in range(lo + r, hi - r, step):
                yield (i, i + r)
        else:
            yield (lo, lo + r)

    def sort(lo, hi):
        if hi > lo:
            mid = lo + (hi - lo) // 2
            yield from sort(lo, mid)
            yield from sort(mid + 1, hi)
            yield from merge(lo, hi, 1)

    return tuple(sort(0, PEER_TOPK - 1))


def _top16_sorted(s):
    n = PEER_TOPK
    assert s.shape[0] == n * F32_SUBLANES
    v = [s[k * F32_SUBLANES:(k + 1) * F32_SUBLANES, :] for k in range(n)]

    def exchange(i, j):
        v[i], v[j] = jnp.maximum(v[i], v[j]), jnp.minimum(v[i], v[j])

    for i, j in _sort16_network():
        exchange(i, j)
    shift = F32_SUBLANES // 2
    while shift >= 1:
        other = [pltpu.roll(x, shift, 0) for x in v]
        v = [jnp.maximum(v[k], other[n - 1 - k]) for k in range(n)]
        d = n // 2
        while d >= 1:
            for k in range(n):
                if k & d == 0:
                    exchange(k, k + d)
            d //= 2
        shift //= 2
    return v


def _peer_select_kernel(xt_ref, wq_ref, sk_ref, cnt0_ref, pz0_ref, rank1_ref, p1_ref,
                        qt_ref, s0_ref, tops_ref, *, heads, nkeys):
    tm = qt_ref.shape[1]
    slab_heads = max(1, SELECT_Q_SLAB_ROWS // (2 * nkeys))
    slab_rows = slab_heads * 2 * nkeys

    def project(first_head):
        r = slice(first_head * 2 * nkeys, first_head * 2 * nkeys + slab_rows)
        qt_ref[r, :] = lax.dot_general(wq_ref[:, r], xt_ref[...], (((0,), (0,)), ((), ())),
                                       preferred_element_type=F32)

    project(0)
    for h in range(heads):
        if h % slab_heads == 0 and h + slab_heads < heads:
            project(h + slab_heads)
        for p in range(2):
            r0 = (2 * h + p) * nkeys
            qhp = qt_ref[r0:r0 + nkeys, :].astype(BF16)
            s = jnp.dot(sk_ref[h, p], qhp, preferred_element_type=F32)
            tops = _top16_sorted(s)
            for r in range(PEER_TOPK):
                tops_ref[p, r, h:h + 1, :] = tops[r][0:1, :]
            if p == 0:
                s0_ref[h] = s
            else:
                slabs = []
                for k in range(nkeys // F32_SUBLANES):
                    sk_slab = s[k * F32_SUBLANES:(k + 1) * F32_SUBLANES, :]
                    rank = jnp.zeros(sk_slab.shape, F32)
                    for l in range(PEER_TOPK):
                        rank = jnp.where(sk_slab < tops[l], float(l + 1), rank)
                    slabs.append(rank)
                rank = jnp.concatenate(slabs, axis=0)
                rank1_ref[h] = pltpu.bitcast(rank.astype(BF16), jnp.uint32)
                p1_ref[h] = pltpu.bitcast(
                    jnp.exp(s - tops_ref[1, 0, h:h + 1, :]).astype(BF16), jnp.uint32)

    a = [tops_ref[0, r] for r in range(PEER_TOPK)]
    b = [tops_ref[1, r] for r in range(PEER_TOPK)]
    cands = [a[k] + b[l] for k in range(PEER_TOPK) for l in range(PEER_TOPK)
             if (k + 1) * (l + 1) <= PEER_TOPK]
    cur = jnp.full(cands[0].shape, jnp.inf, F32)
    for _ in range(PEER_TOPK):
        nxt = jnp.full(cands[0].shape, -jnp.inf, F32)
        for c in cands:
            nxt = jnp.maximum(nxt, jnp.where(c < cur, c, -jnp.inf))
        cur = nxt
    thr = cur
    top = cands[0]
    z = jnp.zeros(thr.shape, F32)
    for c in cands:
        z = z + jnp.where(c >= thr, jnp.exp(c - top), 0.0)
    inv_z = 1.0 / z
    for h in range(heads):
        s0 = s0_ref[h]
        thr_h = thr[h:h + 1, :]
        b_h = [b[l][h:h + 1, :] for l in range(PEER_TOPK)]
        cnt = jnp.zeros(s0.shape, F32)
        step = PEER_TOPK // 2
        while step >= 1:
            probe = b_h[step - 1]
            for base in range(2 * step, PEER_TOPK, 2 * step):
                probe = jnp.where(cnt == float(base), b_h[base + step - 1], probe)
            cnt = cnt + jnp.where(s0 + probe >= thr_h, float(step), 0.0)
            step //= 2
        cnt = jnp.where(s0 + b_h[PEER_TOPK - 1] >= thr_h, float(PEER_TOPK), cnt)
        cnt0_ref[h] = cnt
        pz0_ref[h] = jnp.exp(s0 - a[0][h:h + 1, :]) * inv_z[h:h + 1, :]


def _peer_select(xt, wq, sk):
    d, n = xt.shape
    heads, _, nkeys, _ = sk.shape
    tm = _pick_tile(n, (256, 128))
    key_spec = pl.BlockSpec((heads, nkeys, tm), lambda t: (0, 0, t))
    pair_spec = pl.BlockSpec((heads, nkeys // 2, tm), lambda t: (0, 0, t))
    return pl.pallas_call(
        functools.partial(_peer_select_kernel, heads=heads, nkeys=nkeys),
        out_shape=(jax.ShapeDtypeStruct((heads, nkeys, n), F32),
                   jax.ShapeDtypeStruct((heads, nkeys, n), F32),
                   jax.ShapeDtypeStruct((heads, nkeys // 2, n), jnp.uint32),
                   jax.ShapeDtypeStruct((heads, nkeys // 2, n), jnp.uint32)),
        grid=(n // tm,),
        in_specs=[
            pl.BlockSpec((d, tm), lambda t: (0, t)),
            pl.BlockSpec(wq.shape, lambda t: (0, 0), pipeline_mode=pl.Buffered(1)),
            pl.BlockSpec(sk.shape, lambda t: (0, 0, 0, 0)),
        ],
        out_specs=(key_spec, key_spec, pair_spec, pair_spec),
        scratch_shapes=[pltpu.VMEM((wq.shape[1], tm), F32),
                        pltpu.VMEM((heads, nkeys, tm), F32),
                        pltpu.VMEM((2, PEER_TOPK, heads, tm), F32)],
        compiler_params=_params("parallel"),
        name="peer_select",
    )(xt, wq, sk)


def _peer_expert_kernel(xt_ref, u_ref, v_ref, cnt0_ref, pz0_ref, rank1_ref, p1_ref,
                        out_ref, at0_ref, at1_ref, wt0_ref, wt1_ref, cnt_s, pz_s,
                        *, heads, nkeys, n_blocks, n_items):
    s = pl.program_id(0)
    eb, tm = at0_ref.shape
    rows = eb // nkeys

    @pl.when(s == 0)
    def _():
        at1_ref[...] = jnp.zeros_like(at1_ref)
        wt0_ref[...] = jnp.zeros_like(wt0_ref)

    @pl.when((s <= 2) | (lax.rem(s - 2, n_blocks) == 0))
    def _():
        out_ref[...] = jnp.zeros_like(out_ref)

    item_b = jnp.clip(s - 1, 0, n_items - 1)
    valid_b = jnp.where((s >= 1) & (s <= n_items), 1.0, 0.0).astype(F32)
    first_row = pl.multiple_of(lax.rem(item_b, n_blocks) * rows, rows)

    d_model = out_ref.shape[0]
    reps = nkeys // BF16_SUBLANES
    lane_tiles = tm // LANES

    def routing_piece(at_r, wt_w, ii, tc):
        keys = slice(ii * nkeys, (ii + 1) * nkeys)
        lanes = slice(tc * LANES, (tc + 1) * LANES)
        g = jnp.zeros((nkeys, LANES), BF16)
        for h in range(heads):
            cnt = pltpu.bitcast(pltpu.repeat(cnt_s[h, ii, :, lanes], reps, axis=0), BF16)
            pz = pltpu.bitcast(pltpu.repeat(pz_s[h, ii, :, lanes], reps, axis=0), BF16)
            r1 = pltpu.bitcast(rank1_ref[h, :, lanes], BF16)
            p1 = pltpu.bitcast(p1_ref[h, :, lanes], BF16)
            g = g + jnp.where(r1 < cnt, p1 * pz, jnp.zeros_like(p1))
        wt_w[keys, lanes] = (_gelu_tanh(at_r[keys, lanes]) * g.astype(F32)).astype(BF16)

    def stage_a_piece(at_w, slab, kc):
        r = slice(slab * MM_SLAB, (slab + 1) * MM_SLAB)
        k = slice(kc * MM_DEPTH, (kc + 1) * MM_DEPTH)
        part = jnp.dot(u_ref[r, k], xt_ref[k, :], preferred_element_type=F32)
        if kc == 0:
            at_w[r, :] = part
        else:
            at_w[r, :] += part

    def stage_c_piece(wt_r, slab, kc):
        r = slice(slab * MM_SLAB, (slab + 1) * MM_SLAB)
        k = slice(kc * MM_DEPTH, (kc + 1) * MM_DEPTH)
        out_ref[r, :] += lax.dot_general(v_ref[k, r], wt_r[k, :], (((0,), (0,)), ((), ())),
                                         preferred_element_type=F32)

    def step_interleaved(at_w, at_r, wt_w, wt_r):
        b_pieces = [(ii, tc) for ii in range(rows) for tc in range(lane_tiles)]
        a_pieces = [(slab, kc) for slab in range(eb // MM_SLAB)
                    for kc in range(d_model // MM_DEPTH)]
        c_pieces = [(slab, kc) for slab in range(d_model // MM_SLAB)
                    for kc in range(eb // MM_DEPTH)]
        mxu_pieces = []
        for i in range(max(len(a_pieces), len(c_pieces))):
            if i < len(c_pieces):
                mxu_pieces.append(("c", c_pieces[i]))
            if i < len(a_pieces):
                mxu_pieces.append(("a", a_pieces[i]))
        n_slots = max(len(b_pieces), len(mxu_pieces))
        for slot in range(n_slots):
            for kind, (slab, kc) in mxu_pieces[slot * len(mxu_pieces) // n_slots:
                                               (slot + 1) * len(mxu_pieces) // n_slots]:
                if kind == "a":
                    stage_a_piece(at_w, slab, kc)
                else:
                    stage_c_piece(wt_r, slab, kc)
            for ii, tc in b_pieces[slot * len(b_pieces) // n_slots:
                                   (slot + 1) * len(b_pieces) // n_slots]:
                if tc == 0:
                    stage_row(ii)
                routing_piece(at_r, wt_w, ii, tc)

    def stage_row(ii):
        for h in range(heads):
            cnt_row = cnt0_ref[h, pl.ds(first_row, rows), :][ii:ii + 1, :]
            pz_row = pz0_ref[h, pl.ds(first_row, rows), :][ii:ii + 1, :] * valid_b
            cnt_s[h, ii] = pltpu.bitcast(
                jnp.broadcast_to(cnt_row, (BF16_SUBLANES, tm)).astype(BF16), jnp.uint32)
            pz_s[h, ii] = pltpu.bitcast(
                jnp.broadcast_to(pz_row, (BF16_SUBLANES, tm)).astype(BF16), jnp.uint32)

    @pl.when(lax.rem(s, 2) == 0)
    def _():
        step_interleaved(at0_ref, at1_ref, wt1_ref, wt0_ref)

    @pl.when(lax.rem(s, 2) == 1)
    def _():
        step_interleaved(at1_ref, at0_ref, wt0_ref, wt1_ref)


def _peer_expert(xt, u, vt, cnt0, pz0, rank1, p1, *, layer):
    d, n = xt.shape
    n_exp = u.shape[1]
    heads, nkeys, _ = cnt0.shape
    tm = _pick_tile(n, (512, 256, 128))
    eb = PEER_EXPERT_ROWS * nkeys
    n_blocks = n_exp // eb
    n_items = (n // tm) * n_blocks

    def item(s, lag):
        return jnp.clip(s - lag, 0, n_items - 1)

    tile_b = lambda s: item(s, 1) // n_blocks
    row_spec = pl.BlockSpec((heads, nkeys, tm), lambda s: (0, 0, tile_b(s)))
    pair_spec = pl.BlockSpec((heads, nkeys // 2, tm), lambda s: (0, 0, tile_b(s)))
    return pl.pallas_call(
        functools.partial(_peer_expert_kernel, heads=heads, nkeys=nkeys,
                          n_blocks=n_blocks, n_items=n_items),
        out_shape=jax.ShapeDtypeStruct((d, n), F32),
        grid=(n_items + 2,),
        in_specs=[
            pl.BlockSpec((d, tm), lambda s: (0, item(s, 0) // n_blocks)),
            pl.BlockSpec((None, eb, d), lambda s: (layer, item(s, 0) % n_blocks, 0)),
            pl.BlockSpec((None, eb, d), lambda s: (layer, item(s, 2) % n_blocks, 0)),
            row_spec, row_spec, pair_spec, pair_spec,
        ],
        out_specs=pl.BlockSpec((d, tm), lambda s: (0, item(s, 2) // n_blocks)),
        scratch_shapes=[pltpu.VMEM((eb, tm), F32), pltpu.VMEM((eb, tm), F32),
                        pltpu.VMEM((eb, tm), BF16), pltpu.VMEM((eb, tm), BF16),
                        pltpu.VMEM((heads, PEER_EXPERT_ROWS, F32_SUBLANES, tm), jnp.uint32),
                        pltpu.VMEM((heads, PEER_EXPERT_ROWS, F32_SUBLANES, tm), jnp.uint32)],
        compiler_params=pltpu.CompilerParams(
            dimension_semantics=("arbitrary",), vmem_limit_bytes=PEER_EXPERT_VMEM_LIMIT_BYTES),
        name="peer_expert",
    )(xt, u, vt, cnt0, pz0, rank1, p1)


def _peer(x, xt, w_q, sub_keys, exp_u_all, exp_vt_all, ln_g, ln_b, *, alpha, layer):
    sk = sub_keys.astype(BF16)
    cnt0, pz0, rank1, p1 = _peer_select(xt, w_q.astype(BF16), sk)
    out_t = _peer_expert(xt, exp_u_all, exp_vt_all, cnt0, pz0, rank1, p1, layer=layer)
    return _res_ln(x, out_t, ln_g, ln_b, alpha=alpha, name=f"peer_ln_{layer}")


def kernel(x_prompt, x_sample, state_mlstm_C, state_mlstm_n, state_mlstm_m,
           w_in_a, b_gate_a, hn_gain_a, w_out_a, w_in_b, b_in_b, lnv_g_b, lnv_b_b,
           w_s_b, b_s_b, w_out_b, ln_mix_g, ln_mix_b, ln_ffn_g, ln_ffn_b,
           peer_w_q, peer_sub_keys, peer_u, peer_v):
    bp, tp, d = x_prompt.shape
    bs, ts, _ = x_sample.shape
    depth = ln_mix_g.shape[0]
    alpha = float((2 * depth) ** 0.25)
    heads = b_gate_a.shape[-1] // 2
    dv = hn_gain_a.shape[-1]
    dk = state_mlstm_n.shape[-1]
    hk, hv = heads * dk, heads * dv
    n_prompt, n_sample = bp * tp, bs * ts
    assert depth == 2 and w_in_a.shape[0] == 1 and w_in_b.shape[0] == 1
    assert ts <= SAMPLE_PAD_LEN and tp % MLSTM_CHUNK == 0

    x = jnp.concatenate([x_prompt.reshape(n_prompt, d), x_sample.reshape(n_sample, d)])
    xb = x.astype(BF16)

    n_main = 2 * hk + 2 * hv
    proj = _matmul(xb, w_in_a[0].astype(BF16), jnp.zeros((w_in_a.shape[-1],), F32),
                   n=n_main, name="mlstm_in_proj")
    w_gate = jnp.pad(w_in_a[0][:, n_main:], ((0, 0), (0, GATE_LANES - 2 * heads)))
    b_gate = jnp.pad(b_gate_a[0], (0, GATE_LANES - 2 * heads)).reshape(1, GATE_LANES)
    gates = _gates(x, w_gate, b_gate)

    zeros_c = jnp.zeros((bp, heads, dk, dv), F32)
    hn_p, c_p, n_p, m_p = _mlstm(
        proj[None], gates[None],
        zeros_c, jnp.zeros((bp, heads, dk), F32), jnp.zeros((bp, heads), F32),
        hn_gain_a[0], bsz=bp, seq_len=tp, heads=heads, dk=dk, dv=dv, chunk=MLSTM_CHUNK,
        name="mlstm_prompt")

    pad_t = SAMPLE_PAD_LEN - ts
    proj_s = jnp.pad(proj[n_prompt:].reshape(bs, ts, n_main), ((0, 0), (0, pad_t), (0, 0)))
    lane = jnp.arange(GATE_LANES)
    pad_gate = jnp.where(lane < heads, NEG_BIG, jnp.where(lane < 2 * heads, POS_BIG, 0.0))
    gates_s = jnp.concatenate(
        [gates[n_prompt:].reshape(bs, ts, GATE_LANES),
         jnp.broadcast_to(pad_gate.astype(F32), (bs, pad_t, GATE_LANES))], axis=1)
    hn_s, c_s, n_s, m_s = _mlstm(
        proj_s, gates_s, state_mlstm_C[0], state_mlstm_n[0], state_mlstm_m[0],
        hn_gain_a[0], bsz=bs, seq_len=SAMPLE_PAD_LEN, heads=heads, dk=dk, dv=dv,
        chunk=SAMPLE_PAD_LEN, name="mlstm_sample")

    hn = jnp.concatenate([hn_p[0], hn_s[:, :ts].reshape(n_sample, hv)])
    x, xt = _matmul_res_ln(hn, w_out_a[0].astype(BF16), x, ln_mix_g[0], ln_mix_b[0],
                           alpha=alpha, name="mlstm_out_proj_ln")
    exp_u_all = peer_u.astype(BF16)
    exp_vt_all = peer_v.astype(BF16)
    x, xb = _peer(x, xt, peer_w_q[0], peer_sub_keys[0], exp_u_all, exp_vt_all,
                  ln_ffn_g[0], ln_ffn_b[0], alpha=alpha, layer=0)

    half = w_in_b.shape[-1] // 2
    groups, chunk = w_s_b.shape[1], w_s_b.shape[2]
    assert tp % chunk == 0 and n_sample % chunk == 0 and chunk % ts == 0
    w_in = w_in_b[0].astype(BF16)
    u = _matmul(xb, w_in, b_in_b[0], n=half, act="gelu", out_dtype=BF16,
                name="gmlp_in_proj_u")
    v = _matmul(xb, w_in, b_in_b[0], col_start=half, act="gelu", name="gmlp_in_proj_v")
    ws = jnp.where(jnp.tril(jnp.ones((chunk, chunk), dtype=bool)), w_s_b[0], 0.0)
    ws_sample = jnp.einsum("ab,gts->gatbs", jnp.eye(chunk // ts, dtype=F32),
                           ws[:, :ts, :ts]).reshape(groups, chunk, chunk)
    wmix = jnp.stack([ws, ws_sample])
    bmix = jnp.stack([b_s_b[0].T, jnp.tile(b_s_b[0][:, :ts].T, (chunk // ts, 1))])
    prod, v_s = _gmlp_mix(u, v, wmix, bmix, lnv_g_b[0], lnv_b_b[0],
                          n_prompt_chunks=n_prompt // chunk, n_sample_rows=n_sample)
    x, xt = _matmul_res_ln(prod, w_out_b[0].astype(BF16), x, ln_mix_g[1], ln_mix_b[1],
                           alpha=alpha, name="gmlp_out_proj_ln")
    x, xb = _peer(x, xt, peer_w_q[1], peer_sub_keys[1], exp_u_all, exp_vt_all,
                  ln_ffn_g[1], ln_ffn_b[1], alpha=alpha, layer=1)

    return (x[:n_prompt].reshape(bp, tp, d), x[n_prompt:].reshape(bs, ts, d),
            c_p[None], n_p[None], m_p[None], c_s[None], n_s[None], m_s[None],
            v_s.reshape(1, bs, ts, half))
```

```python
import functools
import math

import jax
import jax.numpy as jnp
from jax import lax
from jax.experimental import pallas as pl
from jax.experimental.pallas import tpu as pltpu

F32 = jnp.float32
BF16 = jnp.bfloat16

LN_EPS = 1e-5
PEER_TOPK = 16
MLSTM_CHUNK = 256
SAMPLE_PAD_LEN = 8
GATE_LANES = 128
LANES = 128
F32_SUBLANES = 8
BF16_SUBLANES = 16
V7X_VMEM_LIMIT_BYTES = 48 * 1024 * 1024
PEER_EXPERT_VMEM_LIMIT_BYTES = 56 * 1024 * 1024
PEER_EXPERT_ROWS = 8
SELECT_Q_SLAB_ROWS = 512
MM_SLAB = 512
MM_DEPTH = 256
NEG_BIG = -1e30
POS_BIG = 1e30


def _params(*sem):
    return pltpu.CompilerParams(dimension_semantics=sem,
                                vmem_limit_bytes=V7X_VMEM_LIMIT_BYTES)


def _pick_tile(n, candidates):
    for c in candidates:
        if n % c == 0:
            return c
    raise ValueError(f"no tile in {candidates} divides {n}")


def _layer_norm(z, g, b):
    mu = jnp.mean(z, axis=-1, keepdims=True)
    zc = z - mu
    var = jnp.mean(zc * zc, axis=-1, keepdims=True)
    return zc * lax.rsqrt(var + LN_EPS) * g + b


def _gelu_tanh(x):
    k0 = -2.0 * math.sqrt(2.0 / math.pi) * math.log2(math.e)
    k1 = 0.044715 * k0
    return x / (1.0 + jnp.exp2(x * (x * x * k1 + k0)))


def _log_sigmoid(x):
    return jnp.minimum(x, 0.0) - jnp.log1p(jnp.exp(-jnp.abs(x)))


def _mm_kernel(x_ref, w_ref, b_ref, o_ref):
    acc = jnp.dot(x_ref[...], w_ref[...], preferred_element_type=F32)
    o_ref[...] = (acc + b_ref[...]).astype(o_ref.dtype)


def _matmul(x, w, bias, *, col_start=0, n=None, out_dtype=F32, name):
    m, k = x.shape
    n = w.shape[1] - col_start if n is None else n
    tm = _pick_tile(m, (512, 256, 128))
    tn = _pick_tile(math.gcd(n, col_start) if col_start else n, (1024, 512, 256, 128))
    first = col_start // tn
    return pl.pallas_call(
        _mm_kernel,
        out_shape=jax.ShapeDtypeStruct((m, n), out_dtype),
        grid=(n // tn, m // tm),
        in_specs=[
            pl.BlockSpec((tm, k), lambda j, i: (i, 0)),
            pl.BlockSpec((k, tn), lambda j, i: (0, first + j)),
            pl.BlockSpec((1, tn), lambda j, i: (0, first + j)),
        ],
        out_specs=pl.BlockSpec((tm, tn), lambda j, i: (i, j)),
        compiler_params=_params("parallel", "parallel"),
        name=name,
    )(x, w, bias.reshape(1, w.shape[1]).astype(F32))


def _gate_kernel(x_ref, w_ref, b_ref, o_ref):
    x = x_ref[...]
    w = w_ref[...]
    xh = x.astype(BF16)
    xl = (x - xh.astype(F32)).astype(BF16)
    wh = w.astype(BF16)
    wl = (w - wh.astype(F32)).astype(BF16)
    acc = jnp.dot(xh, wh, preferred_element_type=F32)
    acc = acc + jnp.dot(xl, wh, preferred_element_type=F32)
    acc = acc + jnp.dot(xh, wl, preferred_element_type=F32)
    o_ref[...] = acc + b_ref[...]


def _gates(x, w_gate, b_gate):
    m, k = x.shape
    tm = _pick_tile(m, (512, 256, 128))
    return pl.pallas_call(
        _gate_kernel,
        out_shape=jax.ShapeDtypeStruct((m, GATE_LANES), F32),
        grid=(m // tm,),
        in_specs=[
            pl.BlockSpec((tm, k), lambda i: (i, 0)),
            pl.BlockSpec((k, GATE_LANES), lambda i: (0, 0)),
            pl.BlockSpec((1, GATE_LANES), lambda i: (0, 0)),
        ],
        out_specs=pl.BlockSpec((tm, GATE_LANES), lambda i: (i, 0)),
        compiler_params=_params("parallel"),
        name="mlstm_gates",
    )(x, w_gate, b_gate)


def _mlstm_kernel(q_ref, k_ref, v_ref, o_ref, g_ref, gt_ref, c0_ref, n0_ref,
                  m0_ref, gain_ref, hn_ref, c_ref, n_ref, m_ref,
                  *, heads, dk, dv, chunk):
    @pl.when(pl.program_id(1) == 0)
    def _():
        c_ref[...] = c0_ref[...]
        n_ref[...] = n0_ref[...]
        m_ref[...] = m0_ref[...]

    n_seq = g_ref.shape[0]
    row = lax.broadcasted_iota(jnp.int32, (chunk, chunk), 0)
    col = lax.broadcasted_iota(jnp.int32, (chunk, chunk), 1)
    causal = row >= col
    tril = causal.astype(F32)
    triu = (row <= col).astype(F32)
    g = [g_ref[b] for b in range(n_seq)]
    gt = [gt_ref[b] for b in range(n_seq)]
    b_col_all = [jnp.dot(tril, _log_sigmoid(x), precision=lax.Precision.HIGHEST,
                         preferred_element_type=F32) for x in g]
    b_row_all = [jnp.dot(_log_sigmoid(x), triu, precision=lax.Precision.HIGHEST,
                         preferred_element_type=F32) for x in gt]
    k_scale = dk ** -0.5

    def gate_terms(b, h):
        bc = b_col_all[b][:, heads + h:heads + h + 1]
        br = b_row_all[b][heads + h:heads + h + 1, :]
        ic = g[b][:, h:h + 1]
        ir = gt[b][h:h + 1, :]
        m_prev = m_ref[b, h][:, 0:1]
        dlog = jnp.where(causal, bc - br + ir, -jnp.inf)
        inter = bc + m_prev
        m_t = jnp.maximum(inter, jnp.max(dlog, axis=-1, keepdims=True))
        b_end = bc[chunk - 1:chunk, :]
        m_new = jnp.maximum(b_end + m_prev,
                            jnp.max(b_end - br + ir, axis=-1, keepdims=True))
        return dict(decay_mat=jnp.exp(dlog - m_t), w_inter=jnp.exp(inter - m_t),
                    floor=jnp.exp(-m_t), m_new=m_new,
                    w_k=jnp.exp(b_end - bc + ic - m_new),
                    decay=jnp.exp(b_end + m_prev - m_new))

    def load_qkv(b, h):
        qh = q_ref[b, :, h * dk:(h + 1) * dk]
        kh = k_ref[b, :, h * dk:(h + 1) * dk] * k_scale
        vb = v_ref[b, :, h * dv:(h + 1) * dv].astype(BF16)
        return qh, kh, vb

    def scores(qh, kh):
        return lax.dot_general(qh.astype(BF16), kh.astype(BF16), (((1,), (1,)), ((), ())),
                               preferred_element_type=F32)

    def read_out(b, h, qh, vb, qk, t):
        c_prev = c_ref[b, h]
        n_prev = n_ref[b, h]
        s = qk * t["decay_mat"]
        num = jnp.dot(s.astype(BF16), vb, preferred_element_type=F32)
        num = num + t["w_inter"] * jnp.dot(qh.astype(BF16), c_prev.astype(BF16),
                                           preferred_element_type=F32)
        den = jnp.sum(s, axis=-1, keepdims=True)
        den = den + t["w_inter"] * jnp.sum(qh * n_prev, axis=-1, keepdims=True)
        return num / jnp.maximum(jnp.abs(den), t["floor"])

    def gate_and_norm(b, h, hval):
        hg = hval * jax.nn.sigmoid(o_ref[b, :, h * dv:(h + 1) * dv])
        mu = jnp.mean(hg, axis=-1, keepdims=True)
        hc = hg - mu
        var = jnp.mean(hc * hc, axis=-1, keepdims=True)
        hn = hc * lax.rsqrt(var + LN_EPS) * gain_ref[h]
        hn_ref[b, :, h * dv:(h + 1) * dv] = hn.astype(hn_ref.dtype)

    def update_state(b, h, kh, vb, t):
        kw = kh * t["w_k"]
        c_ref[b, h] = t["decay"] * c_ref[b, h] + lax.dot_general(
            kw.astype(BF16), vb, (((0,), (0,)), ((), ())), preferred_element_type=F32)
        n_ref[b, h] = t["decay"] * n_ref[b, h] + jnp.sum(kw, axis=0, keepdims=True)
        m_ref[b, h] = jnp.broadcast_to(t["m_new"], (1, GATE_LANES))

    units = [(b, h) for b in range(n_seq) for h in range(heads)]
    qkv = [load_qkv(b, h) for b, h in units]
    qks = [scores(qh, kh) for qh, kh, _ in qkv]
    terms = [gate_terms(b, h) for b, h in units]
    hvals = [read_out(b, h, qkv[i][0], qkv[i][2], qks[i], terms[i])
             for i, (b, h) in enumerate(units)]
    for i, (b, h) in enumerate(units):
        update_state(b, h, qkv[i][1], qkv[i][2], terms[i])
    for i, (b, h) in enumerate(units):
        gate_and_norm(b, h, hvals[i])


def _mlstm(proj, gates, c0, n0, m0, gain, *, bsz, seq_len, heads, dk, dv, chunk,
           seqs_per_step=1, name):
    hk, hv = heads * dk, heads * dv
    chunks = seq_len // chunk
    nb = seqs_per_step
    if proj.shape[0] == 1:
        assert nb == 1
        rows = lambda b, c: (0, b * chunks + c)
    else:
        rows = lambda b, c: (b, c)
    gt = jnp.swapaxes(gates[..., :2 * heads], 1, 2)
    n0 = n0.reshape(bsz, heads, 1, dk)
    m0 = jnp.broadcast_to(m0[..., None, None], (bsz, heads, 1, GATE_LANES))
    state_spec = lambda shape: pl.BlockSpec((nb,) + shape, lambda b, c: (b, 0, 0, 0))
    tok_spec = lambda width, col: pl.BlockSpec((nb, chunk, width),
                                               lambda b, c: rows(b, c) + (col,))
    hn, c, n, m = pl.pallas_call(
        functools.partial(_mlstm_kernel, heads=heads, dk=dk, dv=dv, chunk=chunk),
        out_shape=(
            jax.ShapeDtypeStruct((proj.shape[0], bsz * seq_len // proj.shape[0], hv), BF16),
            jax.ShapeDtypeStruct((bsz, heads, dk, dv), F32),
            jax.ShapeDtypeStruct((bsz, heads, 1, dk), F32),
            jax.ShapeDtypeStruct((bsz, heads, 1, GATE_LANES), F32),
        ),
        grid=(bsz // nb, chunks),
        in_specs=[
            tok_spec(hk, 0),
            tok_spec(hk, 1),
            tok_spec(hv, 1),
            tok_spec(hv, 2),
            tok_spec(GATE_LANES, 0),
            pl.BlockSpec((nb, 2 * heads, chunk),
                         lambda b, c: (rows(b, c)[0], 0, rows(b, c)[1])),
            state_spec((heads, dk, dv)),
            state_spec((heads, 1, dk)),
            state_spec((heads, 1, GATE_LANES)),
            pl.BlockSpec((heads, 1, dv), lambda b, c: (0, 0, 0)),
        ],
        out_specs=(
            tok_spec(hv, 0),
            state_spec((heads, dk, dv)),
            state_spec((heads, 1, dk)),
            state_spec((heads, 1, GATE_LANES)),
        ),
        compiler_params=_params("parallel", "arbitrary"),
        name=name,
    )(proj, proj, proj, proj, gates, gt, c0, n0, m0, gain.reshape(heads, 1, dv))
    return hn, c, n.reshape(bsz, heads, dk), m[:, :, 0, 0]


def _mm_res_ln_kernel(a_ref, w_ref, x_ref, g_ref, b_ref, o_ref, obt_ref, *, alpha):
    y = jnp.dot(a_ref[...], w_ref[...], preferred_element_type=F32)
    out = _layer_norm(alpha * x_ref[...] + y, g_ref[...], b_ref[...])
    o_ref[...] = out
    obt_ref[...] = out.T.astype(BF16)


def _matmul_res_ln(a, w, x, g, b, *, alpha, name):
    m, k = a.shape
    d = w.shape[1]
    tm = _pick_tile(m, (256, 128))
    row = lambda i: (i, 0)
    const = lambda i: (0, 0)
    return pl.pallas_call(
        functools.partial(_mm_res_ln_kernel, alpha=alpha),
        out_shape=(jax.ShapeDtypeStruct((m, d), F32),
                   jax.ShapeDtypeStruct((d, m), BF16)),
        grid=(m // tm,),
        in_specs=[
            pl.BlockSpec((tm, k), row),
            pl.BlockSpec((k, d), const, pipeline_mode=pl.Buffered(1)),
            pl.BlockSpec((tm, d), row),
            pl.BlockSpec((1, d), const),
            pl.BlockSpec((1, d), const),
        ],
        out_specs=(pl.BlockSpec((tm, d), row), pl.BlockSpec((d, tm), lambda i: (0, i))),
        compiler_params=_params("parallel"),
        name=name,
    )(a, w, x, g.reshape(1, d), b.reshape(1, d))


def _res_ln_kernel(x_ref, yt_ref, g_ref, b_ref, o_ref, ob_ref, *, alpha):
    out = _layer_norm(alpha * x_ref[...] + yt_ref[...].T, g_ref[...], b_ref[...])
    o_ref[...] = out
    ob_ref[...] = out.astype(BF16)


def _res_ln(x, yt, g, b, *, alpha, name):
    m, d = x.shape
    tm = _pick_tile(m, (256, 128))
    row = lambda i: (i, 0)
    const = lambda i: (0, 0)
    return pl.pallas_call(
        functools.partial(_res_ln_kernel, alpha=alpha),
        out_shape=(jax.ShapeDtypeStruct((m, d), F32),
                   jax.ShapeDtypeStruct((m, d), BF16)),
        grid=(m // tm,),
        in_specs=[pl.BlockSpec((tm, d), row), pl.BlockSpec((d, tm), lambda i: (0, i)),
                  pl.BlockSpec((1, d), const), pl.BlockSpec((1, d), const)],
        out_specs=(pl.BlockSpec((tm, d), row), pl.BlockSpec((tm, d), row)),
        compiler_params=_params("parallel"),
        name=name,
    )(x, yt, g.reshape(1, d), b.reshape(1, d))


def _gmlp_mix_kernel(u_ref, v_ref, wmix_ref, bmix_ref, lg_ref, lb_ref,
                     prod_ref, vln_ref, *, groups, gdim):
    vln = _layer_norm(_gelu_tanh(v_ref[...]), lg_ref[...], lb_ref[...])
    vln_ref[...] = vln
    for g in range(groups):
        sl = slice(g * gdim, (g + 1) * gdim)
        mixed = jnp.dot(wmix_ref[0, g].astype(BF16), vln[:, sl].astype(BF16),
                        preferred_element_type=F32)
        mixed = mixed + bmix_ref[0, :, g:g + 1]
        u = _gelu_tanh(u_ref[:, sl].astype(F32))
        prod_ref[:, sl] = (u * mixed).astype(prod_ref.dtype)


def _gmlp_mix(u, v, wmix, bmix, lnv_g, lnv_b, *, n_prompt_chunks, n_sample_rows):
    n, half = v.shape
    groups, chunk = wmix.shape[1], wmix.shape[2]
    gdim = half // groups
    kind = lambda i: jnp.where(i >= n_prompt_chunks, 1, 0)
    row = lambda i: (i, 0)
    const = lambda i: (0, 0)
    return pl.pallas_call(
        functools.partial(_gmlp_mix_kernel, groups=groups, gdim=gdim),
        out_shape=(jax.ShapeDtypeStruct((n, half), BF16),
                   jax.ShapeDtypeStruct((n_sample_rows, half), F32)),
        grid=(n // chunk,),
        in_specs=[
            pl.BlockSpec((chunk, half), row),
            pl.BlockSpec((chunk, half), row),
            pl.BlockSpec((1, groups, chunk, chunk), lambda i: (kind(i), 0, 0, 0)),
            pl.BlockSpec((1, chunk, groups), lambda i: (kind(i), 0, 0)),
            pl.BlockSpec((1, half), const),
            pl.BlockSpec((1, half), const),
        ],
        out_specs=(
            pl.BlockSpec((chunk, half), row),
            pl.BlockSpec((chunk, half),
                         lambda i: (jnp.maximum(i - n_prompt_chunks, 0), 0)),
        ),
        compiler_params=_params("arbitrary"),
        name="gmlp_mix",
    )(u, v, wmix, bmix, lnv_g.reshape(1, half), lnv_b.reshape(1, half))


def _sort16_network():
    def merge(lo, hi, r):
        step = 2 * r
        if step < hi - lo:
            yield from merge(lo, hi, step)
            yield from merge(lo + r, hi, step)
            for i in range(lo + r, hi - r, step):
                yield (i, i + r)
        else:
            yield (lo, lo + r)

    def sort(lo, hi):
        if hi > lo:
            mid = lo + (hi - lo) // 2
            yield from sort(lo, mid)
            yield from sort(mid + 1, hi)
            yield from merge(lo, hi, 1)

    return tuple(sort(0, PEER_TOPK - 1))


def _top16_sorted(s):
    n = PEER_TOPK
    assert s.shape[0] == n * F32_SUBLANES
    v = [s[k * F32_SUBLANES:(k + 1) * F32_SUBLANES, :] for k in range(n)]

    def exchange(i, j):
        v[i], v[j] = jnp.maximum(v[i], v[j]), jnp.minimum(v[i], v[j])

    for i, j in _sort16_network():
        exchange(i, j)
    shift = F32_SUBLANES // 2
    while shift >= 1:
        other = [pltpu.roll(x, shift, 0) for x in v]
        v = [jnp.maximum(v[k], other[n - 1 - k]) for k in range(n)]
        d = n // 2
        while d >= 1:
            for k in range(n):
                if k & d == 0:
                    exchange(k, k + d)
            d //= 2
        shift //= 2
    return v


def _peer_select_kernel(xt_ref, wq_ref, sk_ref, cnt0_ref, pz0_ref, rank1_ref, p1_ref,
                        qt_ref, s0_ref, tops_ref, *, heads, nkeys):
    tm = qt_ref.shape[1]
    slab_heads = max(1, SELECT_Q_SLAB_ROWS // (2 * nkeys))
    slab_rows = slab_heads * 2 * nkeys

    def project(first_head):
        r = slice(first_head * 2 * nkeys, first_head * 2 * nkeys + slab_rows)
        qt_ref[r, :] = lax.dot_general(wq_ref[:, r], xt_ref[...], (((0,), (0,)), ((), ())),
                                       preferred_element_type=F32)

    project(0)
    for h in range(heads):
        if h % slab_heads == 0 and h + slab_heads < heads:
            project(h + slab_heads)
        for p in range(2):
            r0 = (2 * h + p) * nkeys
            qhp = qt_ref[r0:r0 + nkeys, :].astype(BF16)
            s = jnp.dot(sk_ref[h, p], qhp, preferred_element_type=F32)
            tops = _top16_sorted(s)
            for r in range(PEER_TOPK):
                tops_ref[p, r, h:h + 1, :] = tops[r][0:1, :]
            if p == 0:
                s0_ref[h] = s
            else:
                slabs = []
                for k in range(nkeys // F32_SUBLANES):
                    sk_slab = s[k * F32_SUBLANES:(k + 1) * F32_SUBLANES, :]
                    rank = jnp.zeros(sk_slab.shape, F32)
                    for l in range(PEER_TOPK):
                        rank = jnp.where(sk_slab < tops[l], float(l + 1), rank)
                    slabs.append(rank)
                rank = jnp.concatenate(slabs, axis=0)
                rank1_ref[h] = pltpu.bitcast(rank.astype(BF16), jnp.uint32)
                p1_ref[h] = pltpu.bitcast(
                    jnp.exp(s - tops_ref[1, 0, h:h + 1, :]).astype(BF16), jnp.uint32)

    a = [tops_ref[0, r] for r in range(PEER_TOPK)]
    b = [tops_ref[1, r] for r in range(PEER_TOPK)]
    cands = [a[k] + b[l] for k in range(PEER_TOPK) for l in range(PEER_TOPK)
             if (k + 1) * (l + 1) <= PEER_TOPK]
    cur = jnp.full(cands[0].shape, jnp.inf, F32)
    for _ in range(PEER_TOPK):
        nxt = jnp.full(cands[0].shape, -jnp.inf, F32)
        for c in cands:
            nxt = jnp.maximum(nxt, jnp.where(c < cur, c, -jnp.inf))
        cur = nxt
    thr = cur
    top = cands[0]
    z = jnp.zeros(thr.shape, F32)
    for c in cands:
        z = z + jnp.where(c >= thr, jnp.exp(c - top), 0.0)
    inv_z = 1.0 / z
    for h in range(heads):
        s0 = s0_ref[h]
        thr_h = thr[h:h + 1, :]
        b_h = [b[l][h:h + 1, :] for l in range(PEER_TOPK)]
        cnt = jnp.zeros(s0.shape, F32)
        step = PEER_TOPK // 2
        while step >= 1:
            probe = b_h[step - 1]
            for base in range(2 * step, PEER_TOPK, 2 * step):
                probe = jnp.where(cnt == float(base), b_h[base + step - 1], probe)
            cnt = cnt + jnp.where(s0 + probe >= thr_h, float(step), 0.0)
            step //= 2
        cnt = jnp.where(s0 + b_h[PEER_TOPK - 1] >= thr_h, float(PEER_TOPK), cnt)
        cnt0_ref[h] = cnt
        pz0_ref[h] = jnp.exp(s0 - a[0][h:h + 1, :]) * inv_z[h:h + 1, :]


def _peer_select(xt, wq, sk):
    d, n = xt.shape
    heads, _, nkeys, _ = sk.shape
    tm = _pick_tile(n, (256, 128))
    key_spec = pl.BlockSpec((heads, nkeys, tm), lambda t: (0, 0, t))
    pair_spec = pl.BlockSpec((heads, nkeys // 2, tm), lambda t: (0, 0, t))
    return pl.pallas_call(
        functools.partial(_peer_select_kernel, heads=heads, nkeys=nkeys),
        out_shape=(jax.ShapeDtypeStruct((heads, nkeys, n), F32),
                   jax.ShapeDtypeStruct((heads, nkeys, n), F32),
                   jax.ShapeDtypeStruct((heads, nkeys // 2, n), jnp.uint32),
                   jax.ShapeDtypeStruct((heads, nkeys // 2, n), jnp.uint32)),
        grid=(n // tm,),
        in_specs=[
            pl.BlockSpec((d, tm), lambda t: (0, t)),
            pl.BlockSpec(wq.shape, lambda t: (0, 0), pipeline_mode=pl.Buffered(1)),
            pl.BlockSpec(sk.shape, lambda t: (0, 0, 0, 0)),
        ],
        out_specs=(key_spec, key_spec, pair_spec, pair_spec),
        scratch_shapes=[pltpu.VMEM((wq.shape[1], tm), F32),
                        pltpu.VMEM((heads, nkeys, tm), F32),
                        pltpu.VMEM((2, PEER_TOPK, heads, tm), F32)],
        compiler_params=_params("parallel"),
        name="peer_select",
    )(xt, wq, sk)


def _peer_expert_kernel(xt_ref, u_ref, v_ref, cnt0_ref, pz0_ref, rank1_ref, p1_ref,
                        out_ref, at0_ref, at1_ref, wt0_ref, wt1_ref, cnt_s, pz_s,
                        *, heads, nkeys, n_blocks, n_items):
    s = pl.program_id(0)
    eb, tm = at0_ref.shape
    rows = eb // nkeys

    @pl.when(s == 0)
    def _():
        at1_ref[...] = jnp.zeros_like(at1_ref)
        wt0_ref[...] = jnp.zeros_like(wt0_ref)

    @pl.when((s <= 2) | (lax.rem(s - 2, n_blocks) == 0))
    def _():
        out_ref[...] = jnp.zeros_like(out_ref)

    item_b = jnp.clip(s - 1, 0, n_items - 1)
    valid_b = jnp.where((s >= 1) & (s <= n_items), 1.0, 0.0).astype(F32)
    first_row = pl.multiple_of(lax.rem(item_b, n_blocks) * rows, rows)

    d_model = out_ref.shape[0]
    reps = nkeys // BF16_SUBLANES
    lane_tiles = tm // LANES

    def routing_piece(at_r, wt_w, ii, tc):
        keys = slice(ii * nkeys, (ii + 1) * nkeys)
        lanes = slice(tc * LANES, (tc + 1) * LANES)
        g = jnp.zeros((nkeys, LANES), BF16)
        for h in range(heads):
            cnt = pltpu.bitcast(pltpu.repeat(cnt_s[h, ii, :, lanes], reps, axis=0), BF16)
            pz = pltpu.bitcast(pltpu.repeat(pz_s[h, ii, :, lanes], reps, axis=0), BF16)
            r1 = pltpu.bitcast(rank1_ref[h, :, lanes], BF16)
            p1 = pltpu.bitcast(p1_ref[h, :, lanes], BF16)
            g = g + jnp.where(r1 < cnt, p1 * pz, jnp.zeros_like(p1))
        wt_w[keys, lanes] = (_gelu_tanh(at_r[keys, lanes]) * g.astype(F32)).astype(BF16)

    def stage_a_piece(at_w, slab, kc):
        r = slice(slab * MM_SLAB, (slab + 1) * MM_SLAB)
        k = slice(kc * MM_DEPTH, (kc + 1) * MM_DEPTH)
        part = jnp.dot(u_ref[r, k], xt_ref[k, :], preferred_element_type=F32)
        if kc == 0:
            at_w[r, :] = part
        else:
            at_w[r, :] += part

    def stage_c_piece(wt_r, slab, kc):
        r = slice(slab * MM_SLAB, (slab + 1) * MM_SLAB)
        k = slice(kc * MM_DEPTH, (kc + 1) * MM_DEPTH)
        out_ref[r, :] += lax.dot_general(v_ref[k, r], wt_r[k, :], (((0,), (0,)), ((), ())),
                                         preferred_element_type=F32)

    def step_interleaved(at_w, at_r, wt_w, wt_r):
        b_pieces = [(ii, tc) for ii in range(rows) for tc in range(lane_tiles)]
        a_pieces = [(slab, kc) for slab in range(eb // MM_SLAB)
                    for kc in range(d_model // MM_DEPTH)]
        c_pieces = [(slab, kc) for slab in range(d_model // MM_SLAB)
                    for kc in range(eb // MM_DEPTH)]
        mxu_pieces = []
        for i in range(max(len(a_pieces), len(c_pieces))):
            if i < len(c_pieces):
                mxu_pieces.append(("c", c_pieces[i]))
            if i < len(a_pieces):
                mxu_pieces.append(("a", a_pieces[i]))
        n_slots = max(len(b_pieces), len(mxu_pieces))
        for slot in range(n_slots):
            for kind, (slab, kc) in mxu_pieces[slot * len(mxu_pieces) // n_slots:
                                               (slot + 1) * len(mxu_pieces) // n_slots]:
                if kind == "a":
                    stage_a_piece(at_w, slab, kc)
                else:
                    stage_c_piece(wt_r, slab, kc)
            for ii, tc in b_pieces[slot * len(b_pieces) // n_slots:
                                   (slot + 1) * len(b_pieces) // n_slots]:
                if tc == 0:
                    stage_row(ii)
                routing_piece(at_r, wt_w, ii, tc)

    def stage_row(ii):
        for h in range(heads):
            cnt_row = cnt0_ref[h, pl.ds(first_row, rows), :][ii:ii + 1, :]
            pz_row = pz0_ref[h, pl.ds(first_row, rows), :][ii:ii + 1, :] * valid_b
            cnt_s[h, ii] = pltpu.bitcast(
                jnp.broadcast_to(cnt_row, (BF16_SUBLANES, tm)).astype(BF16), jnp.uint32)
            pz_s[h, ii] = pltpu.bitcast(
                jnp.broadcast_to(pz_row, (BF16_SUBLANES, tm)).astype(BF16), jnp.uint32)

    @pl.when(lax.rem(s, 2) == 0)
    def _():
        step_interleaved(at0_ref, at1_ref, wt1_ref, wt0_ref)

    @pl.when(lax.rem(s, 2) == 1)
    def _():
        step_interleaved(at1_ref, at0_ref, wt0_ref, wt1_ref)


def _peer_expert(xt, u, vt, cnt0, pz0, rank1, p1, *, layer):
    d, n = xt.shape
    n_exp = u.shape[1]
    heads, nkeys, _ = cnt0.shape
    tm = _pick_tile(n, (512, 256, 128))
    eb = PEER_EXPERT_ROWS * nkeys
    n_blocks = n_exp // eb
    n_items = (n // tm) * n_blocks

    def item(s, lag):
        return jnp.clip(s - lag, 0, n_items - 1)

    tile_b = lambda s: item(s, 1) // n_blocks
    row_spec = pl.BlockSpec((heads, nkeys, tm), lambda s: (0, 0, tile_b(s)))
    pair_spec = pl.BlockSpec((heads, nkeys // 2, tm), lambda s: (0, 0, tile_b(s)))
    return pl.pallas_call(
        functools.partial(_peer_expert_kernel, heads=heads, nkeys=nkeys,
                          n_blocks=n_blocks, n_items=n_items),
        out_shape=jax.ShapeDtypeStruct((d, n), F32),
        grid=(n_items + 2,),
        in_specs=[
            pl.BlockSpec((d, tm), lambda s: (0, item(s, 0) // n_blocks)),
            pl.BlockSpec((None, eb, d), lambda s: (layer, item(s, 0) % n_blocks, 0)),
            pl.BlockSpec((None, eb, d), lambda s: (layer, item(s, 2) % n_blocks, 0)),
            row_spec, row_spec, pair_spec, pair_spec,
        ],
        out_specs=pl.BlockSpec((d, tm), lambda s: (0, item(s, 2) // n_blocks)),
        scratch_shapes=[pltpu.VMEM((eb, tm), F32), pltpu.VMEM((eb, tm), F32),
                        pltpu.VMEM((eb, tm), BF16), pltpu.VMEM((eb, tm), BF16),
                        pltpu.VMEM((heads, PEER_EXPERT_ROWS, F32_SUBLANES, tm), jnp.uint32),
                        pltpu.VMEM((heads, PEER_EXPERT_ROWS, F32_SUBLANES, tm), jnp.uint32)],
        compiler_params=pltpu.CompilerParams(
            dimension_semantics=("arbitrary",), vmem_limit_bytes=PEER_EXPERT_VMEM_LIMIT_BYTES),
        name="peer_expert",
    )(xt, u, vt, cnt0, pz0, rank1, p1)


def _peer(x, xt, w_q, sub_keys, exp_u_all, exp_vt_all, ln_g, ln_b, *, alpha, layer):
    sk = sub_keys.astype(BF16)
    cnt0, pz0, rank1, p1 = _peer_select(xt, w_q.astype(BF16), sk)
    out_t = _peer_expert(xt, exp_u_all, exp_vt_all, cnt0, pz0, rank1, p1, layer=layer)
    return _res_ln(x, out_t, ln_g, ln_b, alpha=alpha, name=f"peer_ln_{layer}")


def kernel(x_prompt, x_sample, state_mlstm_C, state_mlstm_n, state_mlstm_m,
           w_in_a, b_gate_a, hn_gain_a, w_out_a, w_in_b, b_in_b, lnv_g_b, lnv_b_b,
           w_s_b, b_s_b, w_out_b, ln_mix_g, ln_mix_b, ln_ffn_g, ln_ffn_b,
           peer_w_q, peer_sub_keys, peer_u, peer_v):
    bp, tp, d = x_prompt.shape
    bs, ts, _ = x_sample.shape
    depth = ln_mix_g.shape[0]
    alpha = float((2 * depth) ** 0.25)
    heads = b_gate_a.shape[-1] // 2
    dv = hn_gain_a.shape[-1]
    dk = state_mlstm_n.shape[-1]
    hk, hv = heads * dk, heads * dv
    n_prompt, n_sample = bp * tp, bs * ts
    assert depth == 2 and w_in_a.shape[0] == 1 and w_in_b.shape[0] == 1
    assert ts <= SAMPLE_PAD_LEN and tp % MLSTM_CHUNK == 0

    x = jnp.concatenate([x_prompt.reshape(n_prompt, d), x_sample.reshape(n_sample, d)])
    xb = x.astype(BF16)

    n_main = 2 * hk + 2 * hv
    proj = _matmul(xb, w_in_a[0].astype(BF16), jnp.zeros((w_in_a.shape[-1],), F32),
                   n=n_main, name="mlstm_in_proj")
    w_gate = jnp.pad(w_in_a[0][:, n_main:], ((0, 0), (0, GATE_LANES - 2 * heads)))
    b_gate = jnp.pad(b_gate_a[0], (0, GATE_LANES - 2 * heads)).reshape(1, GATE_LANES)
    gates = _gates(x, w_gate, b_gate)

    zeros_c = jnp.zeros((bp, heads, dk, dv), F32)
    hn_p, c_p, n_p, m_p = _mlstm(
        proj[None], gates[None],
        zeros_c, jnp.zeros((bp, heads, dk), F32), jnp.zeros((bp, heads), F32),
        hn_gain_a[0], bsz=bp, seq_len=tp, heads=heads, dk=dk, dv=dv, chunk=MLSTM_CHUNK,
        name="mlstm_prompt")

    pad_t = SAMPLE_PAD_LEN - ts
    proj_s = jnp.pad(proj[n_prompt:].reshape(bs, ts, n_main), ((0, 0), (0, pad_t), (0, 0)))
    lane = jnp.arange(GATE_LANES)
    pad_gate = jnp.where(lane < heads, NEG_BIG, jnp.where(lane < 2 * heads, POS_BIG, 0.0))
    gates_s = jnp.concatenate(
        [gates[n_prompt:].reshape(bs, ts, GATE_LANES),
         jnp.broadcast_to(pad_gate.astype(F32), (bs, pad_t, GATE_LANES))], axis=1)
    hn_s, c_s, n_s, m_s = _mlstm(
        proj_s, gates_s, state_mlstm_C[0], state_mlstm_n[0], state_mlstm_m[0],
        hn_gain_a[0], bsz=bs, seq_len=SAMPLE_PAD_LEN, heads=heads, dk=dk, dv=dv,
        chunk=SAMPLE_PAD_LEN, name="mlstm_sample")

    hn = jnp.concatenate([hn_p[0], hn_s[:, :ts].reshape(n_sample, hv)])
    x, xt = _matmul_res_ln(hn, w_out_a[0].astype(BF16), x, ln_mix_g[0], ln_mix_b[0],
                           alpha=alpha, name="mlstm_out_proj_ln")
    exp_u_all = peer_u.astype(BF16)
    exp_vt_all = peer_v.astype(BF16)
    x, xb = _peer(x, xt, peer_w_q[0], peer_sub_keys[0], exp_u_all, exp_vt_all,
                  ln_ffn_g[0], ln_ffn_b[0], alpha=alpha, layer=0)

    half = w_in_b.shape[-1] // 2
    groups, chunk = w_s_b.shape[1], w_s_b.shape[2]
    assert tp % chunk == 0 and n_sample % chunk == 0 and chunk % ts == 0
    w_in = w_in_b[0].astype(BF16)
    u = _matmul(xb, w_in, b_in_b[0], n=half, out_dtype=BF16, name="gmlp_in_proj_u")
    v = _matmul(xb, w_in, b_in_b[0], col_start=half, name="gmlp_in_proj_v")
    ws = jnp.where(jnp.tril(jnp.ones((chunk, chunk), dtype=bool)), w_s_b[0], 0.0)
    ws_sample = jnp.einsum("ab,gts->gatbs", jnp.eye(chunk // ts, dtype=F32),
                           ws[:, :ts, :ts]).reshape(groups, chunk, chunk)
    wmix = jnp.stack([ws, ws_sample])
    bmix = jnp.stack([b_s_b[0].T, jnp.tile(b_s_b[0][:, :ts].T, (chunk // ts, 1))])
    prod, v_s = _gmlp_mix(u, v, wmix, bmix, lnv_g_b[0], lnv_b_b[0],
                          n_prompt_chunks=n_prompt // chunk, n_sample_rows=n_sample)
    x, xt = _matmul_res_ln(prod, w_out_b[0].astype(BF16), x, ln_mix_g[1], ln_mix_b[1],
                           alpha=alpha, name="gmlp_out_proj_ln")
    x, xb = _peer(x, xt, peer_w_q[1], peer_sub_keys[1], exp_u_all, exp_vt_all,
                  ln_ffn_g[1], ln_ffn_b[1], alpha=alpha, layer=1)

    return (x[:n_prompt].reshape(bp, tp, d), x[n_prompt:].reshape(bs, ts, d),
            c_p[None], n_p[None], m_p[None], c_s[None], n_s[None], m_s[None],
            v_s.reshape(1, bs, ts, half))
```
